```python
import math
import jax, jax.numpy as jnp
from jax import lax
import numpy as np

D_MODEL = 1024
BATCH = 16
SEQ = 2048
DEPTH = 1

CTX_LEN = 256
GRID_W = 64
RET_HEADS = 4
RET_DK = 64
RET_DV = 128
RET_CHUNK = 128
MLA_HEADS = 4
MLA_NOPE = 128
MLA_ROPE = 64
MLA_V = 128
Q_LORA = 384
KV_LORA = 256
D_MIX = RET_HEADS * RET_DV + MLA_HEADS * MLA_V
D_FF = 4 * D_MODEL
ROPE_BASE = 10000.0
Q_BLOCK = 128
EPS = 1e-6
IN_SPLITS = (RET_HEADS * RET_DK, RET_HEADS * RET_DK, RET_HEADS * RET_DV, RET_HEADS * RET_DV,
             Q_LORA, KV_LORA, MLA_ROPE)
IN_COLS = sum(IN_SPLITS)
SPLIT_POINTS = tuple(int(v) for v in np.cumsum(IN_SPLITS)[:-1])

kernel_name = "hymba_retention_mla_adaln_prefix_block"


def rms_norm(x, g):
    x32 = x.astype(jnp.float32)
    y = x32 * lax.rsqrt(jnp.mean(x32 * x32, axis=-1, keepdims=True) + EPS)
    return (y * g.astype(jnp.float32)).astype(x.dtype)


def modulate(h, shift, scale):
    return h * (1 + scale) + shift


def axial_rope_tables(rows, dim):
    row = jnp.repeat(jnp.arange(rows, dtype=jnp.float32), GRID_W)
    col = jnp.tile(jnp.arange(GRID_W, dtype=jnp.float32), rows)
    n_freq = dim // 4
    freq = ROPE_BASE ** (-jnp.arange(n_freq, dtype=jnp.float32) / n_freq)
    ang = jnp.concatenate([row[:, None] * freq, col[:, None] * freq], axis=-1)
    return jnp.cos(ang)[:, None, :], jnp.sin(ang)[:, None, :]


def apply_rope(x, cos, sin):
    half = x.shape[-1] // 2
    x1, x2 = x[..., :half], x[..., half:]
    return jnp.concatenate([x1 * cos - x2 * sin, x2 * cos + x1 * sin], axis=-1).astype(x.dtype)


def retention_chunked(q, k, v, log_gamma, s0):
    B, L, H, dk = q.shape
    dv = v.shape[-1]
    n = L // RET_CHUNK
    qc = q.reshape(B, n, RET_CHUNK, H, dk)
    kc = k.reshape(B, n, RET_CHUNK, H, dk)
    vc = v.reshape(B, n, RET_CHUNK, H, dv)
    pos = jnp.arange(RET_CHUNK, dtype=jnp.float32)
    diff = pos[:, None] - pos[None, :]
    dec = jnp.where(diff >= 0, jnp.exp(log_gamma[:, None, None] * jnp.maximum(diff, 0.0)), 0.0)
    scores = jnp.einsum('bnihd,bnjhd->bnhij', qc, kc) * dec
    o_intra = jnp.einsum('bnhij,bnjhe->bnihe', scores, vc)
    w_k = jnp.exp(log_gamma[:, None] * (RET_CHUNK - 1 - pos)[None, :])
    kv = jnp.einsum('bnjhd,hj,bnjhe->bnhde', kc, w_k, vc).astype(jnp.float32)
    chunk_decay = jnp.exp(log_gamma * RET_CHUNK)[:, None, None]

    def step(s, kv_n):
        return chunk_decay * s + kv_n, s

    _, s_prev = lax.scan(step, s0.astype(jnp.float32), jnp.moveaxis(kv, 1, 0))
    w_q = jnp.exp(log_gamma[:, None] * (pos + 1.0)[None, :])
    o_cross = jnp.einsum('bnihd,hi,nbhde->bnihe', qc, w_q, s_prev)
    return (o_intra + o_cross).reshape(B, L, H, dv)


def retention_final_state(k, v, log_gamma):
    L = k.shape[1]
    w = jnp.exp(log_gamma[:, None] * (L - 1 - jnp.arange(L, dtype=jnp.float32))[None, :])
    return jnp.einsum('blhd,hl,blhe->bhde', k, w, v).astype(jnp.float32)


def retention_mix(q, k, v, gate, lg_f, lg_b, g_ret, s_f, s_b):
    B, L, H, dv = v.shape
    o_f = retention_chunked(q, k, v, lg_f, s_f)
    o_b = retention_chunked(q[:, ::-1], k[:, ::-1], v[:, ::-1], lg_b, s_b)[:, ::-1]
    o = (o_f + o_b).astype(jnp.float32)
    mu = jnp.mean(o, axis=-1, keepdims=True)
    var = jnp.mean(jnp.square(o - mu), axis=-1, keepdims=True)
    o = (o - mu) * lax.rsqrt(var + EPS) * g_ret.astype(jnp.float32).reshape(H, dv)
    return (o.reshape(B, L, H * dv) * jax.nn.silu(gate.astype(jnp.float32))).astype(gate.dtype)


def attend(q, k, v):
    s = jnp.einsum('bqhd,bkhd->bhqk', q, k).astype(jnp.float32) * (1.0 / math.sqrt(q.shape[-1]))
    p = jax.nn.softmax(s, axis=-1)
    return jnp.einsum('bhqk,bkhe->bqhe', p.astype(v.dtype), v)


def blocked_attention(q, k, v):
    B, L, H, d = q.shape
    qb = q.reshape(B, L // Q_BLOCK, Q_BLOCK, H, d).swapaxes(0, 1)
    out = lax.map(lambda qi: attend(qi, k, v), qb)
    return out.swapaxes(0, 1).reshape(B, L, H, v.shape[-1])


def head_group_inputs(h, w_in, g_q, w_uq, g_kv, w_ukv):
    B, L, _ = h.shape
    r_q, r_k, r_v, r_g, c_q, c_kv, k_pe = jnp.split(h @ w_in, SPLIT_POINTS, axis=-1)
    r_q = r_q.reshape(B, L, RET_HEADS, RET_DK)
    r_k = r_k.reshape(B, L, RET_HEADS, RET_DK) * (RET_DK ** -0.5)
    r_v = r_v.reshape(B, L, RET_HEADS, RET_DV)
    q = (rms_norm(c_q, g_q) @ w_uq).reshape(B, L, MLA_HEADS, MLA_NOPE + MLA_ROPE)
    kv = (rms_norm(c_kv, g_kv) @ w_ukv).reshape(B, L, MLA_HEADS, MLA_NOPE + MLA_V)
    q_nope, q_pe = q[..., :MLA_NOPE], q[..., MLA_NOPE:]
    k_nope, m_v = kv[..., :MLA_NOPE], kv[..., MLA_NOPE:]
    k_pe = k_pe[:, :, None, :]
    return r_q, r_k, r_v, r_g, q_nope, q_pe, k_nope, k_pe, m_v


def mla_qk(q_nope, q_pe, k_nope, k_pe):
    q = jnp.concatenate([q_nope, q_pe], axis=-1)
    k = jnp.concatenate([k_nope, jnp.broadcast_to(k_pe, k_nope.shape[:-1] + (MLA_ROPE,))], axis=-1)
    return q, k


def sq_relu_mlp(h, w1, w2):
    return jnp.square(jax.nn.relu(h @ w1)) @ w2


def setup_inputs(seed: int = 0) -> dict:
    key = jax.random.key(seed)
    ks = jax.random.split(key, 24)
    f32 = jnp.float32

    def nrm(k, shape, scale):
        return jax.random.normal(k, shape, f32) * scale

    base_logit = jnp.log(2.0 ** (5.0 + jnp.arange(RET_HEADS, dtype=f32)) - 1.0)
    return {
        "x": nrm(ks[0], (BATCH, SEQ, D_MODEL), 1.0),
        "c": nrm(ks[1], (BATCH, D_MODEL), 1.0),
        "ctx": nrm(ks[2], (BATCH, CTX_LEN, D_MODEL), 1.0),
        "c_ctx": nrm(ks[3], (D_MODEL,), 1.0),
        "w_ada": nrm(ks[4], (DEPTH, D_MODEL, 6 * D_MODEL), 0.5 * D_MODEL ** -0.5),
        "b_ada": nrm(ks[5], (DEPTH, 6 * D_MODEL), 0.01),
        "g_attn": 1.0 + nrm(ks[6], (DEPTH, D_MODEL), 0.05),
        "g_ffn": 1.0 + nrm(ks[7], (DEPTH, D_MODEL), 0.05),
        "w_in": nrm(ks[8], (DEPTH, D_MODEL, IN_COLS), D_MODEL ** -0.5),
        "ret_decay_fwd": base_logit + nrm(ks[9], (DEPTH, RET_HEADS), 0.1),
        "ret_decay_bwd": base_logit + nrm(ks[10], (DEPTH, RET_HEADS), 0.1),
        "g_ret": 1.0 + nrm(ks[11], (DEPTH, RET_HEADS * RET_DV), 0.05),
        "g_q_lora": 1.0 + nrm(ks[12], (DEPTH, Q_LORA), 0.05),
        "w_uq": nrm(ks[13], (DEPTH, Q_LORA, MLA_HEADS * (MLA_NOPE + MLA_ROPE)), Q_LORA ** -0.5),
        "g_kv_lora": 1.0 + nrm(ks[14], (DEPTH, KV_LORA), 0.05),
        "w_ukv": nrm(ks[15], (DEPTH, KV_LORA, MLA_HEADS * (MLA_NOPE + MLA_V)), KV_LORA ** -0.5),
        "w_out": nrm(ks[16], (DEPTH, D_MIX, D_MODEL), D_MIX ** -0.5),
        "w_ff1": nrm(ks[17], (DEPTH, D_MODEL, D_FF), D_MODEL ** -0.5),
        "w_ff2": nrm(ks[18], (DEPTH, D_FF, D_MODEL), D_FF ** -0.5),
        "g_final": 1.0 + nrm(ks[19], (D_MODEL,), 0.05),
    }


def reference(x, c, ctx, c_ctx, w_ada, b_ada, g_attn, g_ffn, w_in, ret_decay_fwd, ret_decay_bwd,
              g_ret, g_q_lora, w_uq, g_kv_lora, w_ukv, w_out, w_ff1, w_ff2, g_final):
    B, L, _ = x.shape
    rows = L // GRID_W
    cos, sin = axial_rope_tables(rows, RET_DK)
    for l in range(DEPTH):
        mod = jax.nn.silu(c) @ w_ada[l] + b_ada[l]
        mod_c = jax.nn.silu(c_ctx) @ w_ada[l] + b_ada[l]
        sh_a, sc_a, gt_a, sh_f, sc_f, gt_f = [m[:, None, :] for m in jnp.split(mod, 6, axis=-1)]
        csh_a, csc_a, cgt_a, csh_f, csc_f, cgt_f = jnp.split(mod_c, 6, axis=-1)

        h = modulate(rms_norm(x, g_attn[l]), sh_a, sc_a)
        hc = modulate(rms_norm(ctx, g_attn[l]), csh_a, csc_a)
        rq, rk, rv, rg, qn, qp, kn, kp, mv = head_group_inputs(h, w_in[l], g_q_lora[l], w_uq[l],
                                                                g_kv_lora[l], w_ukv[l])
        rqc, rkc, rvc, rgc, qnc, qpc, knc, kpc, mvc = head_group_inputs(hc, w_in[l], g_q_lora[l], w_uq[l],
                                                                        g_kv_lora[l], w_ukv[l])
        rq, rk = apply_rope(rq, cos, sin), apply_rope(rk, cos, sin)
        qp, kp = apply_rope(qp, cos, sin), apply_rope(kp, cos, sin)

        lg_f = jax.nn.log_sigmoid(ret_decay_fwd[l].astype(jnp.float32))
        lg_b = jax.nn.log_sigmoid(ret_decay_bwd[l].astype(jnp.float32))
        s_f = retention_final_state(rkc, rvc, lg_f)
        s_b = retention_final_state(rkc[:, ::-1], rvc[:, ::-1], lg_b)
        y_ret = retention_mix(rq, rk, rv, rg, lg_f, lg_b, g_ret[l], s_f, s_b)

        q_m, k_m = mla_qk(qn, qp, kn, kp)
        q_mc, k_mc = mla_qk(qnc, qpc, knc, kpc)
        y_mla = blocked_attention(q_m, jnp.concatenate([k_mc, k_m], axis=1),
                                  jnp.concatenate([mvc, mv], axis=1)).reshape(B, L, MLA_HEADS * MLA_V)

        x_mid = x + gt_a * (jnp.concatenate([y_ret, y_mla], axis=-1) @ w_out[l])

        if l + 1 < DEPTH:
            zero_state = jnp.zeros((B, RET_HEADS, RET_DK, RET_DV), jnp.float32)
            y_ret_c = retention_mix(rqc, rkc, rvc, rgc, lg_f, lg_b, g_ret[l], zero_state, zero_state)
            y_mla_c = attend(q_mc, k_mc, mvc).reshape(B, CTX_LEN, MLA_HEADS * MLA_V)
            ctx = ctx + cgt_a * (jnp.concatenate([y_ret_c, y_mla_c], axis=-1) @ w_out[l])
            ctx = ctx + cgt_f * sq_relu_mlp(modulate(rms_norm(ctx, g_ffn[l]), csh_f, csc_f),
                                            w_ff1[l], w_ff2[l])

        x = x_mid + gt_f * sq_relu_mlp(modulate(rms_norm(x_mid, g_ffn[l]), sh_f, sc_f),
                                       w_ff1[l], w_ff2[l])
    return rms_norm(x, g_final)
```

```python
import functools
import math

import jax
import jax.numpy as jnp
from jax import lax
from jax.experimental import pallas as pl
from jax.experimental.pallas import tpu as pltpu

F32 = jnp.float32
BF16 = jnp.bfloat16

D_MODEL = 1024
GRID_W = 64
RET_HEADS = 4
RET_DK = 64
RET_DV = 128
RET_CHUNK = 128
MLA_HEADS = 4
MLA_NOPE = 128
MLA_ROPE = 64
MLA_V = 128
MLA_QK = MLA_NOPE + MLA_ROPE
Q_LORA = 384
KV_LORA = 256
D_FF = 4 * D_MODEL
ROPE_BASE = 10000.0
EPS = 1e-6

_C_RQ = 0
_C_RK = _C_RQ + RET_HEADS * RET_DK
_C_RV = _C_RK + RET_HEADS * RET_DK
_C_RG = _C_RV + RET_HEADS * RET_DV
_C_CQ = _C_RG + RET_HEADS * RET_DV
_C_CKV = _C_CQ + Q_LORA
_C_KPE = _C_CKV + KV_LORA
IN_COLS = _C_KPE + MLA_ROPE
LANES = 128
IN_COLS_PAD = ((IN_COLS + LANES - 1) // LANES) * LANES

PROJ_TILE = 512
ATTN_TILE = 512
TAIL_TILE = 512
FF_CHUNK = 1024
VMEM_LIMIT = 56 * 1024 * 1024


def _rms(x, g):
    return x * lax.rsqrt(jnp.mean(x * x, axis=-1, keepdims=True) + EPS) * g


def _silu(x):
    return x * jax.nn.sigmoid(x)


def _log_sigmoid(x):
    return jnp.minimum(x, 0.0) - jnp.log(1.0 + jnp.exp(-jnp.abs(x)))


def _swap_halves(x):
    lane = lax.broadcasted_iota(jnp.int32, x.shape, 1)
    first = (lane & (RET_DK - 1)) < (RET_DK // 2)
    return jnp.where(first, pltpu.roll(x, LANES - RET_DK // 2, axis=1), pltpu.roll(x, RET_DK // 2, axis=1))


def _rope(x, cos, sin):
    return x * cos + _swap_halves(x) * sin


def _adaln_kernel(c_ref, w_ref, b_ref, o_ref):
    s = _silu(c_ref[...]).astype(BF16)
    o_ref[...] = jnp.dot(s, w_ref[...].astype(BF16), preferred_element_type=F32) + b_ref[...]


def _adaln(cc, w_ada, b_ada):
    rows, d = cc.shape
    n = w_ada.shape[1]
    tn = 1024
    return pl.pallas_call(
        _adaln_kernel,
        out_shape=jax.ShapeDtypeStruct((rows, n), F32),
        grid=(n // tn,),
        in_specs=[pl.BlockSpec((rows, d), lambda j: (0, 0)),
                  pl.BlockSpec((d, tn), lambda j: (0, j)),
                  pl.BlockSpec((1, tn), lambda j: (0, j))],
        out_specs=pl.BlockSpec((rows, tn), lambda j: (0, j)),
        compiler_params=pltpu.CompilerParams(dimension_semantics=("arbitrary",)),
        name="adaln",
    )(cc, w_ada, b_ada)


def _proj_body(latent, x_ref, mod_ref, gattn_ref, win_ref, gq_ref, wuq_ref, gkv_ref, wukv_ref, *rest):
    if latent:
        cos_ref, sin_ref, rq_ref, rk_ref, rv_ref, rg_ref, q_ref, k_ref, v_ref = rest
    else:
        rk_ref, rv_ref, k_ref, v_ref = rest
    x = x_ref[0]
    mod = mod_ref[0]
    h = _rms(x, gattn_ref[...]) * (1.0 + mod[1:2]) + mod[0:1]
    p = jnp.dot(h.astype(BF16), win_ref[...], preferred_element_type=F32)

    if latent:
        cos = cos_ref[...]
        sin = sin_ref[...]
        lane = lax.broadcasted_iota(jnp.int32, cos.shape, 1)
        low = lane < MLA_ROPE
        cos_lo, sin_lo = jnp.where(low, cos, 1.0), jnp.where(low, sin, 0.0)
        cos_hi, sin_hi = jnp.where(low, 1.0, cos), jnp.where(low, 0.0, sin)

    k_scale = RET_DK ** -0.5
    for blk in range(RET_HEADS * RET_DK // LANES):
        rk = p[:, _C_RK + blk * LANES:_C_RK + (blk + 1) * LANES]
        if latent:
            rq = p[:, _C_RQ + blk * LANES:_C_RQ + (blk + 1) * LANES]
            rq_ref[0, :, blk * LANES:(blk + 1) * LANES] = _rope(rq, cos, sin).astype(BF16)
            rk = _rope(rk, cos, sin)
        rk_ref[0, :, blk * LANES:(blk + 1) * LANES] = (rk * k_scale).astype(BF16)
    rv_ref[0] = p[:, _C_RV:_C_RG].astype(BF16)
    if latent:
        rg_ref[0] = p[:, _C_RG:_C_CQ].astype(BF16)

    ckv = _rms(p[:, _C_CKV:_C_KPE], gkv_ref[...]).astype(BF16)
    kv = jnp.dot(ckv, wukv_ref[...], preferred_element_type=F32)
    kpe = p[:, _C_KPE:_C_KPE + LANES]
    if latent:
        kpe = _rope(kpe, cos, sin)
    kpe = kpe[:, :MLA_ROPE].astype(BF16)
    hw = MLA_NOPE + MLA_V
    for hd in range(MLA_HEADS):
        k_ref[0, hd, :, 0:MLA_NOPE] = kv[:, hd * hw:hd * hw + MLA_NOPE].astype(BF16)
        k_ref[0, hd, :, MLA_NOPE:MLA_QK] = kpe
        v_ref[0, :, hd * MLA_V:(hd + 1) * MLA_V] = kv[:, hd * hw + MLA_NOPE:(hd + 1) * hw].astype(BF16)

    if latent:
        cq = _rms(p[:, _C_CQ:_C_CKV], gq_ref[...]).astype(BF16)
        q = jnp.dot(cq, wuq_ref[...], preferred_element_type=F32)
        q_scale = 1.0 / math.sqrt(MLA_QK)
        blocks = []
        for blk in range(MLA_HEADS * MLA_QK // LANES):
            qb = q[:, blk * LANES:(blk + 1) * LANES]
            lo_col = blk * LANES
            pe_lo = (lo_col % MLA_QK) == MLA_NOPE
            pe_hi = ((lo_col + MLA_ROPE) % MLA_QK) == MLA_NOPE
            if pe_lo:
                qb = _rope(qb, cos_lo, sin_lo)
            elif pe_hi:
                qb = _rope(qb, cos_hi, sin_hi)
            blocks.append(qb * q_scale)
        qr = jnp.concatenate(blocks, axis=1)
        for hd in range(MLA_HEADS):
            q_ref[0, hd] = qr[:, hd * MLA_QK:(hd + 1) * MLA_QK].astype(BF16)


def _proj(latent, x, mod, g_attn, w_in, g_q, w_uq, g_kv, w_ukv, cos=None, sin=None):
    B, L, D = x.shape
    tm = PROJ_TILE if latent else L
    nt = L // tm
    hq = MLA_HEADS * MLA_QK
    hkv = MLA_HEADS * (MLA_NOPE + MLA_V)
    const = lambda b, t: (0, 0)
    if latent:
        mod_map = lambda b, t: (b, 0, 0)
    else:
        mod_map = lambda b, t: (0, 0, 0)
    in_specs = [
        pl.BlockSpec((1, tm, D), lambda b, t: (b, t, 0)),
        pl.BlockSpec((1, 6, D), mod_map),
        pl.BlockSpec((1, D), const),
        pl.BlockSpec((D, IN_COLS_PAD), const),
        pl.BlockSpec((1, Q_LORA), const),
        pl.BlockSpec((Q_LORA, hq), const),
        pl.BlockSpec((1, KV_LORA), const),
        pl.BlockSpec((KV_LORA, hkv), const),
    ]
    args = [x, mod, g_attn, w_in, g_q, w_uq, g_kv, w_ukv]
    tok = lambda w: pl.BlockSpec((1, tm, w), lambda b, t: (b, t, 0))
    tok_s = lambda w: jax.ShapeDtypeStruct((B, L, w), BF16)
    head = pl.BlockSpec((1, MLA_HEADS, tm, MLA_QK), lambda b, t: (b, 0, t, 0))
    head_s = jax.ShapeDtypeStruct((B, MLA_HEADS, L, MLA_QK), BF16)
    dk, dv = RET_HEADS * RET_DK, RET_HEADS * RET_DV
    if latent:
        in_specs += [pl.BlockSpec((tm, LANES), lambda b, t: (t, 0)),
                     pl.BlockSpec((tm, LANES), lambda b, t: (t, 0))]
        args += [cos, sin]
        out_specs = [tok(dk), tok(dk), tok(dv), tok(dv), head, head, tok(MLA_HEADS * MLA_V)]
        out_shape = [tok_s(dk), tok_s(dk), tok_s(dv), tok_s(dv), head_s, head_s, tok_s(MLA_HEADS * MLA_V)]
    else:
        out_specs = [tok(dk), tok(dv), head, tok(MLA_HEADS * MLA_V)]
        out_shape = [tok_s(dk), tok_s(dv), head_s, tok_s(MLA_HEADS * MLA_V)]
    return pl.pallas_call(
        functools.partial(_proj_body, latent),
        out_shape=out_shape,
        grid=(B, nt),
        in_specs=in_specs,
        out_specs=out_specs,
        compiler_params=pltpu.CompilerParams(dimension_semantics=("arbitrary", "arbitrary"),
                                             vmem_limit_bytes=VMEM_LIMIT),
        name="proj_latent" if latent else "proj_ctx",
    )(*args)


def _ret_kernel(q_ref, k_ref, v_ref, g_ref, kc_ref, vc_ref, dl_ref, dw_ref, dr_ref, gret_ref,
                o_ref, kv_scr, s_scr):
    C = RET_CHUNK
    n_chunks = q_ref.shape[1] // C
    n_ctx = kc_ref.shape[1] // C
    two_dk = 2 * RET_DK
    two_dv = 2 * RET_DV

    lg_l = _log_sigmoid(dl_ref[0])
    lg_w = _log_sigmoid(dw_ref[0])
    lg_r = _log_sigmoid(dr_ref[0])
    pos = lax.broadcasted_iota(jnp.int32, (C, two_dk), 0).astype(F32)
    wq_f = jnp.exp(lg_l[0:1] * (pos + 1.0))
    wq_b = jnp.exp(lg_l[1:2] * (C - pos))
    wk_f = jnp.exp(lg_l[0:1] * (C - 1.0 - pos))
    wk_b = jnp.exp(lg_l[1:2] * pos)
    ri = lax.broadcasted_iota(jnp.int32, (C, 2 * C), 0)
    cj = lax.broadcasted_iota(jnp.int32, (C, 2 * C), 1) & (C - 1)
    diff = (ri - cj).astype(F32)
    dec = (jnp.where(diff >= 0, jnp.exp(lg_w[0:1] * jnp.maximum(diff, 0.0)), 0.0)
           + jnp.where(diff <= 0, jnp.exp(lg_w[1:2] * jnp.maximum(-diff, 0.0)), 0.0))
    gamma = jnp.exp(lg_r * float(C))

    srow = lax.broadcasted_iota(jnp.int32, (2 * two_dk, two_dv), 0)
    scol = lax.broadcasted_iota(jnp.int32, (2 * two_dk, two_dv), 1)
    diag = ((srow & (two_dk - 1)) < RET_DK) == (scol < RET_DV)
    fwd_rows = srow < two_dk
    k_lo = lax.broadcasted_iota(jnp.int32, (C, two_dk), 1) < RET_DK
    v_lo = lax.broadcasted_iota(jnp.int32, (C, two_dv), 1) < RET_DV

    def chunk_kv(kch, vch):
        kf = kch.astype(F32)
        kcat = jnp.concatenate([(kf * wk_f).astype(BF16), (kf * wk_b).astype(BF16)], axis=1)
        kv = lax.dot_general(kcat, vch, (((0,), (0,)), ((), ())), preferred_element_type=F32)
        return jnp.where(diag, kv, 0.0)

    ctx_kv = [chunk_kv(kc_ref[0, c * C:(c + 1) * C, :], vc_ref[0, c * C:(c + 1) * C, :]) for c in range(n_ctx)]
    state = jnp.zeros((2 * two_dk, two_dv), F32)
    for s in range(n_ctx):
        state = gamma * state + jnp.where(fwd_rows, ctx_kv[s], ctx_kv[n_ctx - 1 - s])

    def kv_step(n, carry):
        r = pl.multiple_of(n * C, C)
        kv_scr[n] = chunk_kv(k_ref[0, pl.ds(r, C), :], v_ref[0, pl.ds(r, C), :])
        return carry

    lax.fori_loop(0, n_chunks, kv_step, 0)

    def scan_step(s, st):
        sb = st.astype(BF16)
        s_scr[s, 0:two_dk, :] = sb[0:two_dk]
        s_scr[n_chunks - 1 - s, two_dk:2 * two_dk, :] = sb[two_dk:2 * two_dk]
        return gamma * st + jnp.where(fwd_rows, kv_scr[s], kv_scr[n_chunks - 1 - s])

    lax.fori_loop(0, n_chunks, scan_step, state)

    gret = gret_ref[...]

    def out_step(n, carry):
        r = pl.multiple_of(n * C, C)
        qch = q_ref[0, pl.ds(r, C), :]
        kch = k_ref[0, pl.ds(r, C), :]
        vch = v_ref[0, pl.ds(r, C), :]
        zk = jnp.zeros_like(kch)
        kbd = jnp.concatenate([jnp.where(k_lo, kch, zk), jnp.where(k_lo, zk, kch)], axis=0)
        sc = lax.dot_general(qch, kbd, (((1,), (1,)), ((), ())), preferred_element_type=F32)
        pm = (sc * dec).astype(BF16)
        zv = jnp.zeros_like(vch)
        vbd = jnp.concatenate([jnp.where(v_lo, vch, zv), jnp.where(v_lo, zv, vch)], axis=0)
        qf = qch.astype(F32)
        qcat = jnp.concatenate([(qf * wq_f).astype(BF16), (qf * wq_b).astype(BF16)], axis=1)
        o = (jnp.dot(pm, vbd, preferred_element_type=F32)
             + jnp.dot(qcat, s_scr[n], preferred_element_type=F32))
        gate = g_ref[0, pl.ds(r, C), :].astype(F32)
        for hh in range(2):
            oh = o[:, hh * RET_DV:(hh + 1) * RET_DV]
            mu = jnp.mean(oh, axis=-1, keepdims=True)
            ctr = oh - mu
            var = jnp.mean(ctr * ctr, axis=-1, keepdims=True)
            y = ctr * lax.rsqrt(var + EPS) * gret[:, hh * RET_DV:(hh + 1) * RET_DV]
            y = y * _silu(gate[:, hh * RET_DV:(hh + 1) * RET_DV])
            o_ref[0, pl.ds(r, C), hh * RET_DV:(hh + 1) * RET_DV] = y.astype(BF16)
        return carry

    lax.fori_loop(0, n_chunks, out_step, 0)


def _retention(rq, rk, rv, rg, rkc, rvc, dec_lane, dec_wide, dec_row, g_ret):
    B, L, _ = rq.shape
    Lc = rkc.shape[1]
    pairs = RET_HEADS // 2
    two_dk, two_dv = 2 * RET_DK, 2 * RET_DV
    n_chunks = L // RET_CHUNK
    return pl.pallas_call(
        _ret_kernel,
        out_shape=jax.ShapeDtypeStruct((B, L, RET_HEADS * RET_DV), BF16),
        grid=(B, pairs),
        in_specs=[
            pl.BlockSpec((1, L, two_dk), lambda b, p: (b, 0, p)),
            pl.BlockSpec((1, L, two_dk), lambda b, p: (b, 0, p)),
            pl.BlockSpec((1, L, two_dv), lambda b, p: (b, 0, p)),
            pl.BlockSpec((1, L, two_dv), lambda b, p: (b, 0, p)),
            pl.BlockSpec((1, Lc, two_dk), lambda b, p: (b, 0, p)),
            pl.BlockSpec((1, Lc, two_dv), lambda b, p: (b, 0, p)),
            pl.BlockSpec((1, 2, two_dk), lambda b, p: (p, 0, 0)),
            pl.BlockSpec((1, 2, two_dv), lambda b, p: (p, 0, 0)),
            pl.BlockSpec((1, 2 * two_dk, 1), lambda b, p: (p, 0, 0)),
            pl.BlockSpec((1, two_dv), lambda b, p: (0, p)),
        ],
        out_specs=pl.BlockSpec((1, L, two_dv), lambda b, p: (b, 0, p)),
        scratch_shapes=[pltpu.VMEM((n_chunks, 2 * two_dk, two_dv), F32),
                        pltpu.VMEM((n_chunks, 2 * two_dk, two_dv), BF16)],
        compiler_params=pltpu.CompilerParams(dimension_semantics=("arbitrary", "arbitrary"),
                                             vmem_limit_bytes=VMEM_LIMIT),
        name="retention",
    )(rq, rk, rv, rg, rkc, rvc, dec_lane, dec_wide, dec_row, g_ret)


def _attn_kernel(q_ref, kx_ref, kc_ref, vx_ref, vc_ref, o_ref):
    q = q_ref[0, 0]
    nt = (((1,), (1,)), ((), ()))
    sx = lax.dot_general(q, kx_ref[0, 0], nt, preferred_element_type=F32)
    sc = lax.dot_general(q, kc_ref[0, 0], nt, preferred_element_type=F32)
    m = jnp.maximum(jnp.max(sx, axis=-1, keepdims=True), jnp.max(sc, axis=-1, keepdims=True))
    px = jnp.exp(sx - m)
    pc = jnp.exp(sc - m)
    denom = jnp.sum(px, axis=-1, keepdims=True) + jnp.sum(pc, axis=-1, keepdims=True)
    o = (jnp.dot(px.astype(BF16), vx_ref[0], preferred_element_type=F32)
         + jnp.dot(pc.astype(BF16), vc_ref[0], preferred_element_type=F32))
    o_ref[0] = (o / denom).astype(BF16)


def _attention(q, kx, kc, vx, vc):
    B, H, L, dqk = q.shape
    Lc = kc.shape[2]
    tq = ATTN_TILE
    return pl.pallas_call(
        _attn_kernel,
        out_shape=jax.ShapeDtypeStruct((B, L, H * MLA_V), BF16),
        grid=(B, H, L // tq),
        in_specs=[
            pl.BlockSpec((1, 1, tq, dqk), lambda b, h, i: (b, h, i, 0)),
            pl.BlockSpec((1, 1, L, dqk), lambda b, h, i: (b, h, 0, 0)),
            pl.BlockSpec((1, 1, Lc, dqk), lambda b, h, i: (b, h, 0, 0)),
            pl.BlockSpec((1, L, MLA_V), lambda b, h, i: (b, 0, h)),
            pl.BlockSpec((1, Lc, MLA_V), lambda b, h, i: (b, 0, h)),
        ],
        out_specs=pl.BlockSpec((1, tq, MLA_V), lambda b, h, i: (b, i, h)),
        compiler_params=pltpu.CompilerParams(dimension_semantics=("arbitrary", "arbitrary", "arbitrary"),
                                             vmem_limit_bytes=VMEM_LIMIT),
        name="attention",
    )(q, kx, kc, vx, vc)


def _tail_kernel(x_ref, yr_ref, ym_ref, mod_ref, gffn_ref, gfin_ref, wo_r_ref, wo_m_ref, w1_ref, w2_ref, o_ref):
    x = x_ref[0]
    mod = mod_ref[0]
    mix = (jnp.dot(yr_ref[0], wo_r_ref[...], preferred_element_type=F32)
           + jnp.dot(ym_ref[0], wo_m_ref[...], preferred_element_type=F32))
    x_mid = x + mod[2:3] * mix
    h = (_rms(x_mid, gffn_ref[...]) * (1.0 + mod[4:5]) + mod[3:4]).astype(BF16)
    acc = jnp.zeros_like(x_mid)
    for c in range(D_FF // FF_CHUNK):
        u = jnp.dot(h, w1_ref[:, c * FF_CHUNK:(c + 1) * FF_CHUNK], preferred_element_type=F32)
        r = jnp.square(jnp.maximum(u, 0.0)).astype(BF16)
        acc = acc + jnp.dot(r, w2_ref[c * FF_CHUNK:(c + 1) * FF_CHUNK, :], preferred_element_type=F32)
    x_out = x_mid + mod[5:6] * acc
    o_ref[0] = _rms(x_out, gfin_ref[...])


def _tail(x, y_ret, y_mla, mod, g_ffn, g_final, wo_r, wo_m, w1, w2):
    B, L, D = x.shape
    tm = TAIL_TILE
    const = lambda b, t: (0, 0)
    resident = lambda shape: pl.BlockSpec(shape, const, pipeline_mode=pl.Buffered(1))
    dmix = y_ret.shape[2]
    return pl.pallas_call(
        _tail_kernel,
        out_shape=jax.ShapeDtypeStruct((B, L, D), F32),
        grid=(B, L // tm),
        in_specs=[
            pl.BlockSpec((1, tm, D), lambda b, t: (b, t, 0)),
            pl.BlockSpec((1, tm, dmix), lambda b, t: (b, t, 0)),
            pl.BlockSpec((1, tm, dmix), lambda b, t: (b, t, 0)),
            pl.BlockSpec((1, 6, D), lambda b, t: (b, 0, 0)),
            pl.BlockSpec((1, D), const),
            pl.BlockSpec((1, D), const),
            resident(wo_r.shape),
            resident(wo_m.shape),
            resident(w1.shape),
            resident(w2.shape),
        ],
        out_specs=pl.BlockSpec((1, tm, D), lambda b, t: (b, t, 0)),
        compiler_params=pltpu.CompilerParams(dimension_semantics=("arbitrary", "arbitrary"),
                                             vmem_limit_bytes=VMEM_LIMIT),
        name="tail",
    )(x, y_ret, y_mla, mod, g_ffn, g_final, wo_r, wo_m, w1, w2)


def _rope_tables(L):
    rows = L // GRID_W
    row = jnp.repeat(jnp.arange(rows, dtype=F32), GRID_W)
    col = jnp.tile(jnp.arange(GRID_W, dtype=F32), rows)
    n_freq = RET_DK // 4
    freq = ROPE_BASE ** (-jnp.arange(n_freq, dtype=F32) / n_freq)
    ang = jnp.concatenate([row[:, None] * freq, col[:, None] * freq], axis=-1)
    cos, sin = jnp.cos(ang), jnp.sin(ang)
    reps = LANES // RET_DK
    return (jnp.tile(jnp.concatenate([cos, cos], axis=-1), (1, reps)),
            jnp.tile(jnp.concatenate([-sin, sin], axis=-1), (1, reps)))


def _decay_layouts(dec_f, dec_b):
    pairs = RET_HEADS // 2
    d = jnp.stack([dec_f, dec_b]).astype(F32).reshape(2, pairs, 2)
    d = jnp.transpose(d, (1, 0, 2))
    lane = jnp.repeat(d, RET_DK, axis=2)
    wide = jnp.repeat(d, RET_DV, axis=2)
    row = lane.reshape(pairs, 4 * RET_DK, 1)
    return lane, wide, row


def kernel(x, c, ctx, c_ctx, w_ada, b_ada, g_attn, g_ffn, w_in, ret_decay_fwd, ret_decay_bwd, g_ret,
           g_q_lora, w_uq, g_kv_lora, w_ukv, w_out, w_ff1, w_ff2, g_final):
    B, L, D = x.shape
    depth = w_ada.shape[0]
    assert depth == 1, "context-stream update between layers is not implemented"
    cos, sin = _rope_tables(L)
    row2 = lambda v: v.reshape(1, -1)
    l = 0
    rows = ((B + 1 + 7) // 8) * 8
    cc = jnp.zeros((rows, D), F32).at[:B].set(c).at[B].set(c_ctx)
    mod = _adaln(cc, w_ada[l], row2(b_ada[l]))
    mod_x = mod[:B].reshape(B, 6, D)
    mod_c = mod[B:B + 1].reshape(1, 6, D)

    w_in_p = jnp.pad(w_in[l], ((0, 0), (0, IN_COLS_PAD - IN_COLS))).astype(BF16)
    w_uq_b = w_uq[l].astype(BF16)
    w_ukv_b = w_ukv[l].astype(BF16)
    common = (row2(g_attn[l]), w_in_p, row2(g_q_lora[l]), w_uq_b, row2(g_kv_lora[l]), w_ukv_b)
    rq, rk, rv, rg, q, kx, vx = _proj(True, x, mod_x, *common, cos, sin)
    rkc, rvc, kc, vc = _proj(False, ctx, mod_c, *common)

    dec_lane, dec_wide, dec_row = _decay_layouts(ret_decay_fwd[l], ret_decay_bwd[l])
    y_ret = _retention(rq, rk, rv, rg, rkc, rvc, dec_lane, dec_wide, dec_row, row2(g_ret[l]))
    y_mla = _attention(q, kx, kc, vx, vc)

    d_ret = RET_HEADS * RET_DV
    w_out_b = w_out[l].astype(BF16)
    return _tail(x, y_ret, y_mla, mod_x, row2(g_ffn[l]), row2(g_final), w_out_b[:d_ret], w_out_b[d_ret:],
                 w_ff1[l].astype(BF16), w_ff2[l].astype(BF16))
```

```python
import functools
import math

import jax
import jax.numpy as jnp
from jax import lax
from jax.experimental import pallas as pl
from jax.experimental.pallas import tpu as pltpu

F32 = jnp.float32
BF16 = jnp.bfloat16

D_MODEL = 1024
GRID_W = 64
RET_HEADS = 4
RET_DK = 64
RET_DV = 128
RET_CHUNK = 128
MLA_HEADS = 4
MLA_NOPE = 128
MLA_ROPE = 64
MLA_V = 128
MLA_QK = MLA_NOPE + MLA_ROPE
Q_LORA = 384
KV_LORA = 256
D_FF = 4 * D_MODEL
ROPE_BASE = 10000.0
EPS = 1e-6

_C_RQ = 0
_C_RK = _C_RQ + RET_HEADS * RET_DK
_C_RV = _C_RK + RET_HEADS * RET_DK
_C_RG = _C_RV + RET_HEADS * RET_DV
_C_CQ = _C_RG + RET_HEADS * RET_DV
_C_CKV = _C_CQ + Q_LORA
_C_KPE = _C_CKV + KV_LORA
IN_COLS = _C_KPE + MLA_ROPE
LANES = 128
IN_COLS_PAD = ((IN_COLS + LANES - 1) // LANES) * LANES

PROJ_TILE = 512
ATTN_TILE = 1024
ATTN_SUB = 256
TAIL_TILE = 512
FF_CHUNK = 1024
VMEM_LIMIT = 56 * 1024 * 1024


def _rms(x, g):
    return x * lax.rsqrt(jnp.mean(x * x, axis=-1, keepdims=True) + EPS) * g


def _silu(x):
    return x * jax.nn.sigmoid(x)


def _log_sigmoid(x):
    return jnp.minimum(x, 0.0) - jnp.log(1.0 + jnp.exp(-jnp.abs(x)))


def _swap_halves(x):
    lane = lax.broadcasted_iota(jnp.int32, x.shape, 1)
    first = (lane & (RET_DK - 1)) < (RET_DK // 2)
    return jnp.where(first, pltpu.roll(x, LANES - RET_DK // 2, axis=1), pltpu.roll(x, RET_DK // 2, axis=1))


def _rope(x, cos, sin):
    return x * cos + _swap_halves(x) * sin


def _adaln_kernel(c_ref, w_ref, b_ref, o_ref):
    s = _silu(c_ref[...]).astype(BF16)
    o_ref[...] = jnp.dot(s, w_ref[...].astype(BF16), preferred_element_type=F32) + b_ref[...]


def _adaln(cc, w_ada, b_ada):
    rows, d = cc.shape
    n = w_ada.shape[1]
    tn = 1024
    return pl.pallas_call(
        _adaln_kernel,
        out_shape=jax.ShapeDtypeStruct((rows, n), F32),
        grid=(n // tn,),
        in_specs=[pl.BlockSpec((rows, d), lambda j: (0, 0)),
                  pl.BlockSpec((d, tn), lambda j: (0, j)),
                  pl.BlockSpec((1, tn), lambda j: (0, j))],
        out_specs=pl.BlockSpec((rows, tn), lambda j: (0, j)),
        compiler_params=pltpu.CompilerParams(dimension_semantics=("arbitrary",)),
        name="adaln",
    )(cc, w_ada, b_ada)


def _proj_body(latent, x_ref, mod_ref, gattn_ref, win_ref, gq_ref, wuq_ref, gkv_ref, wuk_ref, wuvt_ref, *rest):
    if latent:
        cos_ref, sin_ref, rq_ref, rk_ref, rv_ref, rg_ref, q_ref, k_ref, vt_ref = rest
    else:
        rk_ref, rv_ref, k_ref, vt_ref = rest
    x = x_ref[0]
    mod = mod_ref[0]
    h = _rms(x, gattn_ref[...]) * (1.0 + mod[1:2]) + mod[0:1]
    p = jnp.dot(h.astype(BF16), win_ref[...], preferred_element_type=F32)

    if latent:
        cos = cos_ref[...]
        sin = sin_ref[...]
        lane = lax.broadcasted_iota(jnp.int32, cos.shape, 1)
        low = lane < MLA_ROPE
        cos_lo, sin_lo = jnp.where(low, cos, 1.0), jnp.where(low, sin, 0.0)
        cos_hi, sin_hi = jnp.where(low, 1.0, cos), jnp.where(low, 0.0, sin)

    k_scale = RET_DK ** -0.5
    for blk in range(RET_HEADS * RET_DK // LANES):
        rk = p[:, _C_RK + blk * LANES:_C_RK + (blk + 1) * LANES]
        if latent:
            rq = p[:, _C_RQ + blk * LANES:_C_RQ + (blk + 1) * LANES]
            rq_ref[0, :, blk * LANES:(blk + 1) * LANES] = _rope(rq, cos, sin).astype(BF16)
            rk = _rope(rk, cos, sin)
        rk_ref[0, :, blk * LANES:(blk + 1) * LANES] = (rk * k_scale).astype(BF16)
    rv_ref[0] = p[:, _C_RV:_C_RG].astype(BF16)
    if latent:
        rg_ref[0] = p[:, _C_RG:_C_CQ].astype(BF16)

    ckv = _rms(p[:, _C_CKV:_C_KPE], gkv_ref[...]).astype(BF16)
    kn = jnp.dot(ckv, wuk_ref[...], preferred_element_type=F32)
    vt_ref[0] = lax.dot_general(wuvt_ref[...], ckv, (((1,), (1,)), ((), ())),
                                preferred_element_type=F32).astype(BF16)
    kpe = p[:, _C_KPE:_C_KPE + LANES]
    if latent:
        kpe = _rope(kpe, cos, sin)
    kpe = kpe[:, :MLA_ROPE].astype(BF16)
    for hd in range(MLA_HEADS):
        k_ref[0, hd, :, 0:MLA_NOPE] = kn[:, hd * MLA_NOPE:(hd + 1) * MLA_NOPE].astype(BF16)
        k_ref[0, hd, :, MLA_NOPE:MLA_QK] = kpe

    if latent:
        cq = _rms(p[:, _C_CQ:_C_CKV], gq_ref[...]).astype(BF16)
        q = jnp.dot(cq, wuq_ref[...], preferred_element_type=F32)
        q_scale = math.log2(math.e) / math.sqrt(MLA_QK)
        blocks = []
        for blk in range(MLA_HEADS * MLA_QK // LANES):
            qb = q[:, blk * LANES:(blk + 1) * LANES]
            lo_col = blk * LANES
            pe_lo = (lo_col % MLA_QK) == MLA_NOPE
            pe_hi = ((lo_col + MLA_ROPE) % MLA_QK) == MLA_NOPE
            if pe_lo:
                qb = _rope(qb, cos_lo, sin_lo)
            elif pe_hi:
                qb = _rope(qb, cos_hi, sin_hi)
            blocks.append(qb * q_scale)
        qr = jnp.concatenate(blocks, axis=1)
        for hd in range(MLA_HEADS):
            q_ref[0, hd] = qr[:, hd * MLA_QK:(hd + 1) * MLA_QK].astype(BF16)


def _proj(latent, x, mod, g_attn, w_in, g_q, w_uq, g_kv, w_uk, w_uvt, cos=None, sin=None):
    B, L, D = x.shape
    tm = PROJ_TILE if latent else L
    nt = L // tm
    hq = MLA_HEADS * MLA_QK
    hv = MLA_HEADS * MLA_V
    const = lambda b, t: (0, 0)
    if latent:
        mod_map = lambda b, t: (b, 0, 0)
    else:
        mod_map = lambda b, t: (0, 0, 0)
    in_specs = [
        pl.BlockSpec((1, tm, D), lambda b, t: (b, t, 0)),
        pl.BlockSpec((1, 6, D), mod_map),
        pl.BlockSpec((1, D), const),
        pl.BlockSpec((D, IN_COLS_PAD), const),
        pl.BlockSpec((1, Q_LORA), const),
        pl.BlockSpec((Q_LORA, hq), const),
        pl.BlockSpec((1, KV_LORA), const),
        pl.BlockSpec((KV_LORA, MLA_HEADS * MLA_NOPE), const),
        pl.BlockSpec((hv, KV_LORA), const),
    ]
    args = [x, mod, g_attn, w_in, g_q, w_uq, g_kv, w_uk, w_uvt]
    tok = lambda w: pl.BlockSpec((1, tm, w), lambda b, t: (b, t, 0))
    tok_s = lambda w: jax.ShapeDtypeStruct((B, L, w), BF16)
    head = pl.BlockSpec((1, MLA_HEADS, tm, MLA_QK), lambda b, t: (b, 0, t, 0))
    head_s = jax.ShapeDtypeStruct((B, MLA_HEADS, L, MLA_QK), BF16)
    vt = pl.BlockSpec((1, hv, tm), lambda b, t: (b, 0, t))
    vt_s = jax.ShapeDtypeStruct((B, hv, L), BF16)
    dk, dv = RET_HEADS * RET_DK, RET_HEADS * RET_DV
    if latent:
        in_specs += [pl.BlockSpec((tm, LANES), lambda b, t: (t, 0)),
                     pl.BlockSpec((tm, LANES), lambda b, t: (t, 0))]
        args += [cos, sin]
        out_specs = [tok(dk), tok(dk), tok(dv), tok(dv), head, head, vt]
        out_shape = [tok_s(dk), tok_s(dk), tok_s(dv), tok_s(dv), head_s, head_s, vt_s]
    else:
        out_specs = [tok(dk), tok(dv), head, vt]
        out_shape = [tok_s(dk), tok_s(dv), head_s, vt_s]
    return pl.pallas_call(
        functools.partial(_proj_body, latent),
        out_shape=out_shape,
        grid=(B, nt),
        in_specs=in_specs,
        out_specs=out_specs,
        compiler_params=pltpu.CompilerParams(dimension_semantics=("arbitrary", "arbitrary"),
                                             vmem_limit_bytes=VMEM_LIMIT),
        name="proj_latent" if latent else "proj_ctx",
    )(*args)


def _ret_kernel(q_ref, k_ref, v_ref, g_ref, kc_ref, vc_ref, dl_ref, dw_ref, dr_ref, gret_ref,
                o_ref, kv_scr, s_scr):
    C = RET_CHUNK
    n_chunks = q_ref.shape[1] // C
    n_ctx = kc_ref.shape[1] // C
    two_dk = 2 * RET_DK
    two_dv = 2 * RET_DV

    lg_l = _log_sigmoid(dl_ref[0])
    lg_w = _log_sigmoid(dw_ref[0])
    lg_r = _log_sigmoid(dr_ref[0])
    pos = lax.broadcasted_iota(jnp.int32, (C, two_dk), 0).astype(F32)
    wq_f = jnp.exp(lg_l[0:1] * (pos + 1.0))
    wq_b = jnp.exp(lg_l[1:2] * (C - pos))
    wk_f = jnp.exp(lg_l[0:1] * (C - 1.0 - pos))
    wk_b = jnp.exp(lg_l[1:2] * pos)
    ri = lax.broadcasted_iota(jnp.int32, (C, 2 * C), 0)
    cj = lax.broadcasted_iota(jnp.int32, (C, 2 * C), 1) & (C - 1)
    diff = (ri - cj).astype(F32)
    dec = (jnp.where(diff >= 0, jnp.exp(lg_w[0:1] * jnp.maximum(diff, 0.0)), 0.0)
           + jnp.where(diff <= 0, jnp.exp(lg_w[1:2] * jnp.maximum(-diff, 0.0)), 0.0))
    gamma = jnp.exp(lg_r * float(C))

    srow = lax.broadcasted_iota(jnp.int32, (2 * two_dk, two_dv), 0)
    scol = lax.broadcasted_iota(jnp.int32, (2 * two_dk, two_dv), 1)
    diag = ((srow & (two_dk - 1)) < RET_DK) == (scol < RET_DV)
    fwd_rows = srow < two_dk
    k_lo = lax.broadcasted_iota(jnp.int32, (C, two_dk), 1) < RET_DK
    v_lo = lax.broadcasted_iota(jnp.int32, (C, two_dv), 1) < RET_DV

    def chunk_kv(kch, vch):
        kf = kch.astype(F32)
        kcat = jnp.concatenate([(kf * wk_f).astype(BF16), (kf * wk_b).astype(BF16)], axis=1)
        kv = lax.dot_general(kcat, vch, (((0,), (0,)), ((), ())), preferred_element_type=F32)
        return jnp.where(diag, kv, 0.0)

    ctx_kv = [chunk_kv(kc_ref[0, c * C:(c + 1) * C, :], vc_ref[0, c * C:(c + 1) * C, :]) for c in range(n_ctx)]
    state = jnp.zeros((2 * two_dk, two_dv), F32)
    for s in range(n_ctx):
        state = gamma * state + jnp.where(fwd_rows, ctx_kv[s], ctx_kv[n_ctx - 1 - s])

    def kv_step(n, carry):
        r = pl.multiple_of(n * C, C)
        kv_scr[n] = chunk_kv(k_ref[0, pl.ds(r, C), :], v_ref[0, pl.ds(r, C), :])
        return carry

    lax.fori_loop(0, n_chunks, kv_step, 0)

    def scan_step(s, st):
        sb = st.astype(BF16)
        s_scr[s, 0:two_dk, :] = sb[0:two_dk]
        s_scr[n_chunks - 1 - s, two_dk:2 * two_dk, :] = sb[two_dk:2 * two_dk]
        return gamma * st + jnp.where(fwd_rows, kv_scr[s], kv_scr[n_chunks - 1 - s])

    lax.fori_loop(0, n_chunks, scan_step, state)

    gret = gret_ref[...]

    def out_step(n, carry):
        r = pl.multiple_of(n * C, C)
        qch = q_ref[0, pl.ds(r, C), :]
        kch = k_ref[0, pl.ds(r, C), :]
        vch = v_ref[0, pl.ds(r, C), :]
        zk = jnp.zeros_like(kch)
        kbd = jnp.concatenate([jnp.where(k_lo, kch, zk), jnp.where(k_lo, zk, kch)], axis=0)
        sc = lax.dot_general(qch, kbd, (((1,), (1,)), ((), ())), preferred_element_type=F32)
        pm = (sc * dec).astype(BF16)
        zv = jnp.zeros_like(vch)
        vbd = jnp.concatenate([jnp.where(v_lo, vch, zv), jnp.where(v_lo, zv, vch)], axis=0)
        qf = qch.astype(F32)
        qcat = jnp.concatenate([(qf * wq_f).astype(BF16), (qf * wq_b).astype(BF16)], axis=1)
        o = (jnp.dot(pm, vbd, preferred_element_type=F32)
             + jnp.dot(qcat, s_scr[n], preferred_element_type=F32))
        gate = g_ref[0, pl.ds(r, C), :].astype(F32)
        for hh in range(2):
            oh = o[:, hh * RET_DV:(hh + 1) * RET_DV]
            mu = jnp.mean(oh, axis=-1, keepdims=True)
            ctr = oh - mu
            var = jnp.mean(ctr * ctr, axis=-1, keepdims=True)
            y = ctr * lax.rsqrt(var + EPS) * gret[:, hh * RET_DV:(hh + 1) * RET_DV]
            y = y * _silu(gate[:, hh * RET_DV:(hh + 1) * RET_DV])
            o_ref[0, pl.ds(r, C), hh * RET_DV:(hh + 1) * RET_DV] = y.astype(BF16)
        return carry

    lax.fori_loop(0, n_chunks, out_step, 0)


def _retention(rq, rk, rv, rg, rkc, rvc, dec_lane, dec_wide, dec_row, g_ret):
    B, L, _ = rq.shape
    Lc = rkc.shape[1]
    pairs = RET_HEADS // 2
    two_dk, two_dv = 2 * RET_DK, 2 * RET_DV
    n_chunks = L // RET_CHUNK
    return pl.pallas_call(
        _ret_kernel,
        out_shape=jax.ShapeDtypeStruct((B, L, RET_HEADS * RET_DV), BF16),
        grid=(B, pairs),
        in_specs=[
            pl.BlockSpec((1, L, two_dk), lambda b, p: (b, 0, p)),
            pl.BlockSpec((1, L, two_dk), lambda b, p: (b, 0, p)),
            pl.BlockSpec((1, L, two_dv), lambda b, p: (b, 0, p)),
            pl.BlockSpec((1, L, two_dv), lambda b, p: (b, 0, p)),
            pl.BlockSpec((1, Lc, two_dk), lambda b, p: (b, 0, p)),
            pl.BlockSpec((1, Lc, two_dv), lambda b, p: (b, 0, p)),
            pl.BlockSpec((1, 2, two_dk), lambda b, p: (p, 0, 0)),
            pl.BlockSpec((1, 2, two_dv), lambda b, p: (p, 0, 0)),
            pl.BlockSpec((1, 2 * two_dk, 1), lambda b, p: (p, 0, 0)),
            pl.BlockSpec((1, two_dv), lambda b, p: (0, p)),
        ],
        out_specs=pl.BlockSpec((1, L, two_dv), lambda b, p: (b, 0, p)),
        scratch_shapes=[pltpu.VMEM((n_chunks, 2 * two_dk, two_dv), F32),
                        pltpu.VMEM((n_chunks, 2 * two_dk, two_dv), BF16)],
        compiler_params=pltpu.CompilerParams(dimension_semantics=("arbitrary", "arbitrary"),
                                             vmem_limit_bytes=VMEM_LIMIT),
        name="retention",
    )(rq, rk, rv, rg, rkc, rvc, dec_lane, dec_wide, dec_row, g_ret)


def _attn_kernel(q_ref, kx_ref, kc_ref, vtx_ref, vtc_ref, o_ref, s_scr):
    nt = (((1,), (1,)), ((), ()))
    n_sub = q_ref.shape[2] // ATTN_SUB
    lx = kx_ref.shape[2]

    def scores(s):
        q = q_ref[0, 0, s * ATTN_SUB:(s + 1) * ATTN_SUB, :]
        sx = lax.dot_general(kx_ref[0, 0], q, nt, preferred_element_type=F32)
        sc = lax.dot_general(kc_ref[0, 0], q, nt, preferred_element_type=F32)
        s_scr[s % 2, 0:lx] = sx
        s_scr[s % 2, lx:] = sc
        return jnp.maximum(jnp.max(sx, axis=0, keepdims=True), jnp.max(sc, axis=0, keepdims=True))

    m_next = scores(0)
    for s in range(n_sub):
        m = m_next
        if s + 1 < n_sub:
            m_next = scores(s + 1)
        p = jnp.exp2(s_scr[s % 2] - m)
        den = jnp.sum(p, axis=0, keepdims=True)
        pb = p.astype(BF16)
        ot = (jnp.dot(vtx_ref[0], pb[0:lx], preferred_element_type=F32)
              + jnp.dot(vtc_ref[0], pb[lx:], preferred_element_type=F32))
        o_ref[0, s * ATTN_SUB:(s + 1) * ATTN_SUB, :] = (ot * (1.0 / den)).T.astype(BF16)


def _attention(q, kx, kc, vtx, vtc):
    B, H, L, dqk = q.shape
    Lc = kc.shape[2]
    tq = ATTN_TILE
    return pl.pallas_call(
        _attn_kernel,
        out_shape=jax.ShapeDtypeStruct((B, L, H * MLA_V), BF16),
        grid=(B, H, L // tq),
        in_specs=[
            pl.BlockSpec((1, 1, tq, dqk), lambda b, h, i: (b, h, i, 0)),
            pl.BlockSpec((1, 1, L, dqk), lambda b, h, i: (b, h, 0, 0)),
            pl.BlockSpec((1, 1, Lc, dqk), lambda b, h, i: (b, h, 0, 0)),
            pl.BlockSpec((1, MLA_V, L), lambda b, h, i: (b, h, 0)),
            pl.BlockSpec((1, MLA_V, Lc), lambda b, h, i: (b, h, 0)),
        ],
        out_specs=pl.BlockSpec((1, tq, MLA_V), lambda b, h, i: (b, i, h)),
        scratch_shapes=[pltpu.VMEM((2, L + Lc, ATTN_SUB), F32)],
        compiler_params=pltpu.CompilerParams(dimension_semantics=("arbitrary", "arbitrary", "arbitrary"),
                                             vmem_limit_bytes=VMEM_LIMIT),
        name="attention",
    )(q, kx, kc, vtx, vtc)


def _tail_kernel(x_ref, yr_ref, ym_ref, mod_ref, gffn_ref, gfin_ref, wo_r_ref, wo_m_ref, w1_ref, w2_ref, o_ref):
    x = x_ref[0]
    mod = mod_ref[0]
    mix = (jnp.dot(yr_ref[0], wo_r_ref[...], preferred_element_type=F32)
           + jnp.dot(ym_ref[0], wo_m_ref[...], preferred_element_type=F32))
    x_mid = x + mod[2:3] * mix
    h = (_rms(x_mid, gffn_ref[...]) * (1.0 + mod[4:5]) + mod[3:4]).astype(BF16)
    acc = jnp.zeros_like(x_mid)
    for c in range(D_FF // FF_CHUNK):
        u = jnp.dot(h, w1_ref[:, c * FF_CHUNK:(c + 1) * FF_CHUNK], preferred_element_type=F32)
        r = jnp.square(jnp.maximum(u, 0.0)).astype(BF16)
        acc = acc + jnp.dot(r, w2_ref[c * FF_CHUNK:(c + 1) * FF_CHUNK, :], preferred_element_type=F32)
    x_out = x_mid + mod[5:6] * acc
    o_ref[0] = _rms(x_out, gfin_ref[...])


def _tail(x, y_ret, y_mla, mod, g_ffn, g_final, wo_r, wo_m, w1, w2):
    B, L, D = x.shape
    tm = TAIL_TILE
    const = lambda b, t: (0, 0)
    resident = lambda shape: pl.BlockSpec(shape, const, pipeline_mode=pl.Buffered(1))
    dmix = y_ret.shape[2]
    return pl.pallas_call(
        _tail_kernel,
        out_shape=jax.ShapeDtypeStruct((B, L, D), F32),
        grid=(B, L // tm),
        in_specs=[
            pl.BlockSpec((1, tm, D), lambda b, t: (b, t, 0)),
            pl.BlockSpec((1, tm, dmix), lambda b, t: (b, t, 0)),
            pl.BlockSpec((1, tm, dmix), lambda b, t: (b, t, 0)),
            pl.BlockSpec((1, 6, D), lambda b, t: (b, 0, 0)),
            pl.BlockSpec((1, D), const),
            pl.BlockSpec((1, D), const),
            resident(wo_r.shape),
            resident(wo_m.shape),
            resident(w1.shape),
            resident(w2.shape),
        ],
        out_specs=pl.BlockSpec((1, tm, D), lambda b, t: (b, t, 0)),
        compiler_params=pltpu.CompilerParams(dimension_semantics=("arbitrary", "arbitrary"),
                                             vmem_limit_bytes=VMEM_LIMIT),
        name="tail",
    )(x, y_ret, y_mla, mod, g_ffn, g_final, wo_r, wo_m, w1, w2)


def _rope_tables(L):
    rows = L // GRID_W
    row = jnp.repeat(jnp.arange(rows, dtype=F32), GRID_W)
    col = jnp.tile(jnp.arange(GRID_W, dtype=F32), rows)
    n_freq = RET_DK // 4
    freq = ROPE_BASE ** (-jnp.arange(n_freq, dtype=F32) / n_freq)
    ang = jnp.concatenate([row[:, None] * freq, col[:, None] * freq], axis=-1)
    cos, sin = jnp.cos(ang), jnp.sin(ang)
    reps = LANES // RET_DK
    return (jnp.tile(jnp.concatenate([cos, cos], axis=-1), (1, reps)),
            jnp.tile(jnp.concatenate([-sin, sin], axis=-1), (1, reps)))


def _decay_layouts(dec_f, dec_b):
    pairs = RET_HEADS // 2
    d = jnp.stack([dec_f, dec_b]).astype(F32).reshape(2, pairs, 2)
    d = jnp.transpose(d, (1, 0, 2))
    lane = jnp.repeat(d, RET_DK, axis=2)
    wide = jnp.repeat(d, RET_DV, axis=2)
    row = lane.reshape(pairs, 4 * RET_DK, 1)
    return lane, wide, row


def kernel(x, c, ctx, c_ctx, w_ada, b_ada, g_attn, g_ffn, w_in, ret_decay_fwd, ret_decay_bwd, g_ret,
           g_q_lora, w_uq, g_kv_lora, w_ukv, w_out, w_ff1, w_ff2, g_final):
    B, L, D = x.shape
    depth = w_ada.shape[0]
    assert depth == 1, "context-stream update between layers is not implemented"
    cos, sin = _rope_tables(L)
    row2 = lambda v: v.reshape(1, -1)
    l = 0
    rows = ((B + 1 + 7) // 8) * 8
    cc = jnp.zeros((rows, D), F32).at[:B].set(c).at[B].set(c_ctx)
    mod = _adaln(cc, w_ada[l], row2(b_ada[l]))
    mod_x = mod[:B].reshape(B, 6, D)
    mod_c = mod[B:B + 1].reshape(1, 6, D)

    w_in_p = jnp.pad(w_in[l], ((0, 0), (0, IN_COLS_PAD - IN_COLS))).astype(BF16)
    w_uq_b = w_uq[l].astype(BF16)
    w_ukv_h = w_ukv[l].astype(BF16).reshape(KV_LORA, MLA_HEADS, MLA_NOPE + MLA_V)
    w_uk = w_ukv_h[:, :, :MLA_NOPE].reshape(KV_LORA, MLA_HEADS * MLA_NOPE)
    w_uvt = w_ukv_h[:, :, MLA_NOPE:].reshape(KV_LORA, MLA_HEADS * MLA_V).T
    common = (row2(g_attn[l]), w_in_p, row2(g_q_lora[l]), w_uq_b, row2(g_kv_lora[l]), w_uk, w_uvt)
    rq, rk, rv, rg, q, kx, vtx = _proj(True, x, mod_x, *common, cos, sin)
    rkc, rvc, kc, vtc = _proj(False, ctx, mod_c, *common)

    dec_lane, dec_wide, dec_row = _decay_layouts(ret_decay_fwd[l], ret_decay_bwd[l])
    y_ret = _retention(rq, rk, rv, rg, rkc, rvc, dec_lane, dec_wide, dec_row, row2(g_ret[l]))
    y_mla = _attention(q, kx, kc, vtx, vtc)

    d_ret = RET_HEADS * RET_DV
    w_out_b = w_out[l].astype(BF16)
    return _tail(x, y_ret, y_mla, mod_x, row2(g_ffn[l]), row2(g_final), w_out_b[:d_ret], w_out_b[d_ret:],
                 w_ff1[l].astype(BF16), w_ff2[l].astype(BF16))
```

```python
import functools
import math

import jax
import jax.numpy as jnp
from jax import lax
from jax.experimental import pallas as pl
from jax.experimental.pallas import tpu as pltpu

F32 = jnp.float32
BF16 = jnp.bfloat16

D_MODEL = 1024
GRID_W = 64
RET_HEADS = 4
RET_DK = 64
RET_DV = 128
RET_CHUNK = 128
MLA_HEADS = 4
MLA_NOPE = 128
MLA_ROPE = 64
MLA_V = 128
MLA_QK = MLA_NOPE + MLA_ROPE
Q_LORA = 384
KV_LORA = 256
D_FF = 4 * D_MODEL
ROPE_BASE = 10000.0
EPS = 1e-6

_C_RQ = 0
_C_RK = _C_RQ + RET_HEADS * RET_DK
_C_RV = _C_RK + RET_HEADS * RET_DK
_C_RG = _C_RV + RET_HEADS * RET_DV
_C_CQ = _C_RG + RET_HEADS * RET_DV
_C_CKV = _C_CQ + Q_LORA
_C_KPE = _C_CKV + KV_LORA
IN_COLS = _C_KPE + MLA_ROPE
LANES = 128
IN_COLS_PAD = ((IN_COLS + LANES - 1) // LANES) * LANES

PROJ_TILE = 512
ATTN_TILE = 1024
ATTN_SUB = 256
RET_UNROLL = 8
TAIL_TILE = 512
FF_CHUNK = 1024
VMEM_LIMIT = 56 * 1024 * 1024


def _rms(x, g):
    return x * lax.rsqrt(jnp.mean(x * x, axis=-1, keepdims=True) + EPS) * g


def _silu(x):
    return x * jax.nn.sigmoid(x)


def _log_sigmoid(x):
    return jnp.minimum(x, 0.0) - jnp.log(1.0 + jnp.exp(-jnp.abs(x)))


def _swap_halves(x):
    lane = lax.broadcasted_iota(jnp.int32, x.shape, 1)
    first = (lane & (RET_DK - 1)) < (RET_DK // 2)
    return jnp.where(first, pltpu.roll(x, LANES - RET_DK // 2, axis=1), pltpu.roll(x, RET_DK // 2, axis=1))


def _rope(x, cos, sin):
    return x * cos + _swap_halves(x) * sin


def _adaln_kernel(c_ref, w_ref, b_ref, o_ref):
    s = _silu(c_ref[...]).astype(BF16)
    o_ref[...] = jnp.dot(s, w_ref[...].astype(BF16), preferred_element_type=F32) + b_ref[...]


def _adaln(cc, w_ada, b_ada):
    rows, d = cc.shape
    n = w_ada.shape[1]
    tn = 1024
    return pl.pallas_call(
        _adaln_kernel,
        out_shape=jax.ShapeDtypeStruct((rows, n), F32),
        grid=(n // tn,),
        in_specs=[pl.BlockSpec((rows, d), lambda j: (0, 0)),
                  pl.BlockSpec((d, tn), lambda j: (0, j)),
                  pl.BlockSpec((1, tn), lambda j: (0, j))],
        out_specs=pl.BlockSpec((rows, tn), lambda j: (0, j)),
        compiler_params=pltpu.CompilerParams(dimension_semantics=("arbitrary",)),
        name="adaln",
    )(cc, w_ada, b_ada)


def _proj_body(latent, x_ref, mod_ref, gattn_ref, win_ref, gq_ref, wuq_ref, gkv_ref, wuk_ref, wuvt_ref, *rest):
    if latent:
        cos_ref, sin_ref, rq_ref, rk_ref, rv_ref, rg_ref, q_ref, k_ref, vt_ref = rest
    else:
        rk_ref, rv_ref, k_ref, vt_ref = rest
    x = x_ref[0]
    mod = mod_ref[0]
    h = _rms(x, gattn_ref[...]) * (1.0 + mod[1:2]) + mod[0:1]
    p = jnp.dot(h.astype(BF16), win_ref[...], preferred_element_type=F32)

    if latent:
        cos = cos_ref[...]
        sin = sin_ref[...]
        lane = lax.broadcasted_iota(jnp.int32, cos.shape, 1)
        low = lane < MLA_ROPE
        cos_lo, sin_lo = jnp.where(low, cos, 1.0), jnp.where(low, sin, 0.0)
        cos_hi, sin_hi = jnp.where(low, 1.0, cos), jnp.where(low, 0.0, sin)

    k_scale = RET_DK ** -0.5
    for blk in range(RET_HEADS * RET_DK // LANES):
        rk = p[:, _C_RK + blk * LANES:_C_RK + (blk + 1) * LANES]
        if latent:
            rq = p[:, _C_RQ + blk * LANES:_C_RQ + (blk + 1) * LANES]
            rq_ref[0, :, blk * LANES:(blk + 1) * LANES] = _rope(rq, cos, sin).astype(BF16)
            rk = _rope(rk, cos, sin)
        rk_ref[0, :, blk * LANES:(blk + 1) * LANES] = (rk * k_scale).astype(BF16)
    rv_ref[0] = p[:, _C_RV:_C_RG].astype(BF16)
    if latent:
        rg_ref[0] = p[:, _C_RG:_C_CQ].astype(BF16)

    ckv = _rms(p[:, _C_CKV:_C_KPE], gkv_ref[...]).astype(BF16)
    kn = jnp.dot(ckv, wuk_ref[...], preferred_element_type=F32)
    vt_ref[0] = lax.dot_general(wuvt_ref[...], ckv, (((1,), (1,)), ((), ())),
                                preferred_element_type=F32).astype(BF16)
    kpe = p[:, _C_KPE:_C_KPE + LANES]
    if latent:
        kpe = _rope(kpe, cos, sin)
    kpe = kpe[:, :MLA_ROPE].astype(BF16)
    for hd in range(MLA_HEADS):
        k_ref[0, hd, :, 0:MLA_NOPE] = kn[:, hd * MLA_NOPE:(hd + 1) * MLA_NOPE].astype(BF16)
        k_ref[0, hd, :, MLA_NOPE:MLA_QK] = kpe

    if latent:
        cq = _rms(p[:, _C_CQ:_C_CKV], gq_ref[...]).astype(BF16)
        q = jnp.dot(cq, wuq_ref[...], preferred_element_type=F32)
        q_scale = math.log2(math.e) / math.sqrt(MLA_QK)
        blocks = []
        for blk in range(MLA_HEADS * MLA_QK // LANES):
            qb = q[:, blk * LANES:(blk + 1) * LANES]
            lo_col = blk * LANES
            pe_lo = (lo_col % MLA_QK) == MLA_NOPE
            pe_hi = ((lo_col + MLA_ROPE) % MLA_QK) == MLA_NOPE
            if pe_lo:
                qb = _rope(qb, cos_lo, sin_lo)
            elif pe_hi:
                qb = _rope(qb, cos_hi, sin_hi)
            blocks.append(qb * q_scale)
        qr = jnp.concatenate(blocks, axis=1)
        for hd in range(MLA_HEADS):
            q_ref[0, hd] = qr[:, hd * MLA_QK:(hd + 1) * MLA_QK].astype(BF16)


def _proj(latent, x, mod, g_attn, w_in, g_q, w_uq, g_kv, w_uk, w_uvt, cos=None, sin=None):
    B, L, D = x.shape
    tm = PROJ_TILE if latent else L
    nt = L // tm
    hq = MLA_HEADS * MLA_QK
    hv = MLA_HEADS * MLA_V
    const = lambda b, t: (0, 0)
    if latent:
        mod_map = lambda b, t: (b, 0, 0)
    else:
        mod_map = lambda b, t: (0, 0, 0)
    in_specs = [
        pl.BlockSpec((1, tm, D), lambda b, t: (b, t, 0)),
        pl.BlockSpec((1, 6, D), mod_map),
        pl.BlockSpec((1, D), const),
        pl.BlockSpec((D, IN_COLS_PAD), const),
        pl.BlockSpec((1, Q_LORA), const),
        pl.BlockSpec((Q_LORA, hq), const),
        pl.BlockSpec((1, KV_LORA), const),
        pl.BlockSpec((KV_LORA, MLA_HEADS * MLA_NOPE), const),
        pl.BlockSpec((hv, KV_LORA), const),
    ]
    args = [x, mod, g_attn, w_in, g_q, w_uq, g_kv, w_uk, w_uvt]
    tok = lambda w: pl.BlockSpec((1, tm, w), lambda b, t: (b, t, 0))
    tok_s = lambda w: jax.ShapeDtypeStruct((B, L, w), BF16)
    head = pl.BlockSpec((1, MLA_HEADS, tm, MLA_QK), lambda b, t: (b, 0, t, 0))
    head_s = jax.ShapeDtypeStruct((B, MLA_HEADS, L, MLA_QK), BF16)
    vt = pl.BlockSpec((1, hv, tm), lambda b, t: (b, 0, t))
    vt_s = jax.ShapeDtypeStruct((B, hv, L), BF16)
    dk, dv = RET_HEADS * RET_DK, RET_HEADS * RET_DV
    if latent:
        in_specs += [pl.BlockSpec((tm, LANES), lambda b, t: (t, 0)),
                     pl.BlockSpec((tm, LANES), lambda b, t: (t, 0))]
        args += [cos, sin]
        out_specs = [tok(dk), tok(dk), tok(dv), tok(dv), head, head, vt]
        out_shape = [tok_s(dk), tok_s(dk), tok_s(dv), tok_s(dv), head_s, head_s, vt_s]
    else:
        out_specs = [tok(dk), tok(dv), head, vt]
        out_shape = [tok_s(dk), tok_s(dv), head_s, vt_s]
    return pl.pallas_call(
        functools.partial(_proj_body, latent),
        out_shape=out_shape,
        grid=(B, nt),
        in_specs=in_specs,
        out_specs=out_specs,
        compiler_params=pltpu.CompilerParams(dimension_semantics=("arbitrary", "arbitrary"),
                                             vmem_limit_bytes=VMEM_LIMIT),
        name="proj_latent" if latent else "proj_ctx",
    )(*args)


def _ret_kernel(q_ref, k_ref, v_ref, g_ref, kc_ref, vc_ref, dl_ref, dw_ref, dr_ref, gret_ref,
                o_ref, kv_scr, s_scr):
    C = RET_CHUNK
    n_chunks = q_ref.shape[1] // C
    n_ctx = kc_ref.shape[1] // C
    two_dk = 2 * RET_DK
    two_dv = 2 * RET_DV

    lg_l = _log_sigmoid(dl_ref[0])
    lg_w = _log_sigmoid(dw_ref[0])
    lg_r = _log_sigmoid(dr_ref[0])
    pos = lax.broadcasted_iota(jnp.int32, (C, two_dk), 0).astype(F32)
    wq_f = jnp.exp(lg_l[0:1] * (pos + 1.0))
    wq_b = jnp.exp(lg_l[1:2] * (C - pos))
    wk_f = jnp.exp(lg_l[0:1] * (C - 1.0 - pos))
    wk_b = jnp.exp(lg_l[1:2] * pos)
    ri = lax.broadcasted_iota(jnp.int32, (C, 2 * C), 0)
    cj = lax.broadcasted_iota(jnp.int32, (C, 2 * C), 1) & (C - 1)
    diff = (ri - cj).astype(F32)
    dec = (jnp.where(diff >= 0, jnp.exp(lg_w[0:1] * jnp.maximum(diff, 0.0)), 0.0)
           + jnp.where(diff <= 0, jnp.exp(lg_w[1:2] * jnp.maximum(-diff, 0.0)), 0.0))
    gamma = jnp.exp(lg_r * float(C))

    srow = lax.broadcasted_iota(jnp.int32, (2 * two_dk, two_dv), 0)
    scol = lax.broadcasted_iota(jnp.int32, (2 * two_dk, two_dv), 1)
    diag = ((srow & (two_dk - 1)) < RET_DK) == (scol < RET_DV)
    fwd_rows = srow < two_dk
    k_lo = lax.broadcasted_iota(jnp.int32, (C, two_dk), 1) < RET_DK
    v_lo = lax.broadcasted_iota(jnp.int32, (C, two_dv), 1) < RET_DV

    def chunk_kv(kch, vch):
        kf = kch.astype(F32)
        kcat = jnp.concatenate([(kf * wk_f).astype(BF16), (kf * wk_b).astype(BF16)], axis=1)
        kv = lax.dot_general(kcat, vch, (((0,), (0,)), ((), ())), preferred_element_type=F32)
        return jnp.where(diag, kv, 0.0)

    ctx_kv = [chunk_kv(kc_ref[0, c * C:(c + 1) * C, :], vc_ref[0, c * C:(c + 1) * C, :]) for c in range(n_ctx)]
    state = jnp.zeros((2 * two_dk, two_dv), F32)
    for s in range(n_ctx):
        state = gamma * state + jnp.where(fwd_rows, ctx_kv[s], ctx_kv[n_ctx - 1 - s])

    def kv_step(g, carry):
        for u in range(RET_UNROLL):
            n = g * RET_UNROLL + u
            r = pl.multiple_of(n * C, C)
            kv_scr[n] = chunk_kv(k_ref[0, pl.ds(r, C), :], v_ref[0, pl.ds(r, C), :])
        return carry

    lax.fori_loop(0, n_chunks // RET_UNROLL, kv_step, 0)

    def scan_step(s, st):
        sb = st.astype(BF16)
        s_scr[s, 0:two_dk, :] = sb[0:two_dk]
        s_scr[n_chunks - 1 - s, two_dk:2 * two_dk, :] = sb[two_dk:2 * two_dk]
        return gamma * st + jnp.where(fwd_rows, kv_scr[s], kv_scr[n_chunks - 1 - s])

    lax.fori_loop(0, n_chunks, scan_step, state)

    gret = gret_ref[...]

    def out_chunk(n):
        r = pl.multiple_of(n * C, C)
        qch = q_ref[0, pl.ds(r, C), :]
        kch = k_ref[0, pl.ds(r, C), :]
        vch = v_ref[0, pl.ds(r, C), :]
        zk = jnp.zeros_like(kch)
        kbd = jnp.concatenate([jnp.where(k_lo, kch, zk), jnp.where(k_lo, zk, kch)], axis=0)
        sc = lax.dot_general(qch, kbd, (((1,), (1,)), ((), ())), preferred_element_type=F32)
        pm = (sc * dec).astype(BF16)
        zv = jnp.zeros_like(vch)
        vbd = jnp.concatenate([jnp.where(v_lo, vch, zv), jnp.where(v_lo, zv, vch)], axis=0)
        qf = qch.astype(F32)
        qcat = jnp.concatenate([(qf * wq_f).astype(BF16), (qf * wq_b).astype(BF16)], axis=1)
        o = (jnp.dot(pm, vbd, preferred_element_type=F32)
             + jnp.dot(qcat, s_scr[n], preferred_element_type=F32))
        gate = g_ref[0, pl.ds(r, C), :].astype(F32)
        for hh in range(2):
            oh = o[:, hh * RET_DV:(hh + 1) * RET_DV]
            mu = jnp.mean(oh, axis=-1, keepdims=True)
            ctr = oh - mu
            var = jnp.mean(ctr * ctr, axis=-1, keepdims=True)
            y = ctr * lax.rsqrt(var + EPS) * gret[:, hh * RET_DV:(hh + 1) * RET_DV]
            y = y * _silu(gate[:, hh * RET_DV:(hh + 1) * RET_DV])
            o_ref[0, pl.ds(r, C), hh * RET_DV:(hh + 1) * RET_DV] = y.astype(BF16)

    def out_step(g, carry):
        for u in range(RET_UNROLL):
            out_chunk(g * RET_UNROLL + u)
        return carry

    lax.fori_loop(0, n_chunks // RET_UNROLL, out_step, 0)


def _retention(rq, rk, rv, rg, rkc, rvc, dec_lane, dec_wide, dec_row, g_ret):
    B, L, _ = rq.shape
    Lc = rkc.shape[1]
    pairs = RET_HEADS // 2
    two_dk, two_dv = 2 * RET_DK, 2 * RET_DV
    n_chunks = L // RET_CHUNK
    return pl.pallas_call(
        _ret_kernel,
        out_shape=jax.ShapeDtypeStruct((B, L, RET_HEADS * RET_DV), BF16),
        grid=(B, pairs),
        in_specs=[
            pl.BlockSpec((1, L, two_dk), lambda b, p: (b, 0, p)),
            pl.BlockSpec((1, L, two_dk), lambda b, p: (b, 0, p)),
            pl.BlockSpec((1, L, two_dv), lambda b, p: (b, 0, p)),
            pl.BlockSpec((1, L, two_dv), lambda b, p: (b, 0, p)),
            pl.BlockSpec((1, Lc, two_dk), lambda b, p: (b, 0, p)),
            pl.BlockSpec((1, Lc, two_dv), lambda b, p: (b, 0, p)),
            pl.BlockSpec((1, 2, two_dk), lambda b, p: (p, 0, 0)),
            pl.BlockSpec((1, 2, two_dv), lambda b, p: (p, 0, 0)),
            pl.BlockSpec((1, 2 * two_dk, 1), lambda b, p: (p, 0, 0)),
            pl.BlockSpec((1, two_dv), lambda b, p: (0, p)),
        ],
        out_specs=pl.BlockSpec((1, L, two_dv), lambda b, p: (b, 0, p)),
        scratch_shapes=[pltpu.VMEM((n_chunks, 2 * two_dk, two_dv), F32),
                        pltpu.VMEM((n_chunks, 2 * two_dk, two_dv), BF16)],
        compiler_params=pltpu.CompilerParams(dimension_semantics=("arbitrary", "arbitrary"),
                                             vmem_limit_bytes=VMEM_LIMIT),
        name="retention",
    )(rq, rk, rv, rg, rkc, rvc, dec_lane, dec_wide, dec_row, g_ret)


def _attn_kernel(q_ref, kx_ref, kc_ref, vtx_ref, vtc_ref, o_ref, s_scr):
    nt = (((1,), (1,)), ((), ()))
    n_sub = q_ref.shape[2] // ATTN_SUB
    lx = kx_ref.shape[2]

    def scores(s):
        q = q_ref[0, 0, s * ATTN_SUB:(s + 1) * ATTN_SUB, :]
        sx = lax.dot_general(kx_ref[0, 0], q, nt, preferred_element_type=F32)
        sc = lax.dot_general(kc_ref[0, 0], q, nt, preferred_element_type=F32)
        s_scr[s % 2, 0:lx] = sx
        s_scr[s % 2, lx:] = sc
        return jnp.maximum(jnp.max(sx, axis=0, keepdims=True), jnp.max(sc, axis=0, keepdims=True))

    m_next = scores(0)
    for s in range(n_sub):
        m = m_next
        if s + 1 < n_sub:
            m_next = scores(s + 1)
        p = jnp.exp2(s_scr[s % 2] - m)
        den = jnp.sum(p, axis=0, keepdims=True)
        pb = p.astype(BF16)
        ot = (jnp.dot(vtx_ref[0], pb[0:lx], preferred_element_type=F32)
              + jnp.dot(vtc_ref[0], pb[lx:], preferred_element_type=F32))
        o_ref[0, s * ATTN_SUB:(s + 1) * ATTN_SUB, :] = (ot * (1.0 / den)).T.astype(BF16)


def _attention(q, kx, kc, vtx, vtc):
    B, H, L, dqk = q.shape
    Lc = kc.shape[2]
    tq = ATTN_TILE
    return pl.pallas_call(
        _attn_kernel,
        out_shape=jax.ShapeDtypeStruct((B, L, H * MLA_V), BF16),
        grid=(B, H, L // tq),
        in_specs=[
            pl.BlockSpec((1, 1, tq, dqk), lambda b, h, i: (b, h, i, 0)),
            pl.BlockSpec((1, 1, L, dqk), lambda b, h, i: (b, h, 0, 0)),
            pl.BlockSpec((1, 1, Lc, dqk), lambda b, h, i: (b, h, 0, 0)),
            pl.BlockSpec((1, MLA_V, L), lambda b, h, i: (b, h, 0)),
            pl.BlockSpec((1, MLA_V, Lc), lambda b, h, i: (b, h, 0)),
        ],
        out_specs=pl.BlockSpec((1, tq, MLA_V), lambda b, h, i: (b, i, h)),
        scratch_shapes=[pltpu.VMEM((2, L + Lc, ATTN_SUB), F32)],
        compiler_params=pltpu.CompilerParams(dimension_semantics=("arbitrary", "arbitrary", "arbitrary"),
                                             vmem_limit_bytes=VMEM_LIMIT),
        name="attention",
    )(q, kx, kc, vtx, vtc)


def _tail_kernel(x_ref, yr_ref, ym_ref, mod_ref, gffn_ref, gfin_ref, wo_r_ref, wo_m_ref, w1_ref, w2_ref, o_ref):
    x = x_ref[0]
    mod = mod_ref[0]
    mix = (jnp.dot(yr_ref[0], wo_r_ref[...], preferred_element_type=F32)
           + jnp.dot(ym_ref[0], wo_m_ref[...], preferred_element_type=F32))
    x_mid = x + mod[2:3] * mix
    h = (_rms(x_mid, gffn_ref[...]) * (1.0 + mod[4:5]) + mod[3:4]).astype(BF16)
    acc = jnp.zeros_like(x_mid)
    for c in range(D_FF // FF_CHUNK):
        u = jnp.dot(h, w1_ref[:, c * FF_CHUNK:(c + 1) * FF_CHUNK], preferred_element_type=F32)
        r = jnp.square(jnp.maximum(u, 0.0)).astype(BF16)
        acc = acc + jnp.dot(r, w2_ref[c * FF_CHUNK:(c + 1) * FF_CHUNK, :], preferred_element_type=F32)
    x_out = x_mid + mod[5:6] * acc
    o_ref[0] = _rms(x_out, gfin_ref[...])


def _tail(x, y_ret, y_mla, mod, g_ffn, g_final, wo_r, wo_m, w1, w2):
    B, L, D = x.shape
    tm = TAIL_TILE
    const = lambda b, t: (0, 0)
    resident = lambda shape: pl.BlockSpec(shape, const, pipeline_mode=pl.Buffered(1))
    dmix = y_ret.shape[2]
    return pl.pallas_call(
        _tail_kernel,
        out_shape=jax.ShapeDtypeStruct((B, L, D), F32),
        grid=(B, L // tm),
        in_specs=[
            pl.BlockSpec((1, tm, D), lambda b, t: (b, t, 0)),
            pl.BlockSpec((1, tm, dmix), lambda b, t: (b, t, 0)),
            pl.BlockSpec((1, tm, dmix), lambda b, t: (b, t, 0)),
            pl.BlockSpec((1, 6, D), lambda b, t: (b, 0, 0)),
            pl.BlockSpec((1, D), const),
            pl.BlockSpec((1, D), const),
            resident(wo_r.shape),
            resident(wo_m.shape),
            resident(w1.shape),
            resident(w2.shape),
        ],
        out_specs=pl.BlockSpec((1, tm, D), lambda b, t: (b, t, 0)),
        compiler_params=pltpu.CompilerParams(dimension_semantics=("arbitrary", "arbitrary"),
                                             vmem_limit_bytes=VMEM_LIMIT),
        name="tail",
    )(x, y_ret, y_mla, mod, g_ffn, g_final, wo_r, wo_m, w1, w2)


def _rope_tables(L):
    rows = L // GRID_W
    row = jnp.repeat(jnp.arange(rows, dtype=F32), GRID_W)
    col = jnp.tile(jnp.arange(GRID_W, dtype=F32), rows)
    n_freq = RET_DK // 4
    freq = ROPE_BASE ** (-jnp.arange(n_freq, dtype=F32) / n_freq)
    ang = jnp.concatenate([row[:, None] * freq, col[:, None] * freq], axis=-1)
    cos, sin = jnp.cos(ang), jnp.sin(ang)
    reps = LANES // RET_DK
    return (jnp.tile(jnp.concatenate([cos, cos], axis=-1), (1, reps)),
            jnp.tile(jnp.concatenate([-sin, sin], axis=-1), (1, reps)))


def _decay_layouts(dec_f, dec_b):
    pairs = RET_HEADS // 2
    d = jnp.stack([dec_f, dec_b]).astype(F32).reshape(2, pairs, 2)
    d = jnp.transpose(d, (1, 0, 2))
    lane = jnp.repeat(d, RET_DK, axis=2)
    wide = jnp.repeat(d, RET_DV, axis=2)
    row = lane.reshape(pairs, 4 * RET_DK, 1)
    return lane, wide, row


def kernel(x, c, ctx, c_ctx, w_ada, b_ada, g_attn, g_ffn, w_in, ret_decay_fwd, ret_decay_bwd, g_ret,
           g_q_lora, w_uq, g_kv_lora, w_ukv, w_out, w_ff1, w_ff2, g_final):
    B, L, D = x.shape
    depth = w_ada.shape[0]
    assert depth == 1, "context-stream update between layers is not implemented"
    cos, sin = _rope_tables(L)
    row2 = lambda v: v.reshape(1, -1)
    l = 0
    rows = ((B + 1 + 7) // 8) * 8
    cc = jnp.zeros((rows, D), F32).at[:B].set(c).at[B].set(c_ctx)
    mod = _adaln(cc, w_ada[l], row2(b_ada[l]))
    mod_x = mod[:B].reshape(B, 6, D)
    mod_c = mod[B:B + 1].reshape(1, 6, D)

    w_in_p = jnp.pad(w_in[l], ((0, 0), (0, IN_COLS_PAD - IN_COLS))).astype(BF16)
    w_uq_b = w_uq[l].astype(BF16)
    w_ukv_h = w_ukv[l].astype(BF16).reshape(KV_LORA, MLA_HEADS, MLA_NOPE + MLA_V)
    w_uk = w_ukv_h[:, :, :MLA_NOPE].reshape(KV_LORA, MLA_HEADS * MLA_NOPE)
    w_uvt = w_ukv_h[:, :, MLA_NOPE:].reshape(KV_LORA, MLA_HEADS * MLA_V).T
    common = (row2(g_attn[l]), w_in_p, row2(g_q_lora[l]), w_uq_b, row2(g_kv_lora[l]), w_uk, w_uvt)
    rq, rk, rv, rg, q, kx, vtx = _proj(True, x, mod_x, *common, cos, sin)
    rkc, rvc, kc, vtc = _proj(False, ctx, mod_c, *common)

    dec_lane, dec_wide, dec_row = _decay_layouts(ret_decay_fwd[l], ret_decay_bwd[l])
    y_ret = _retention(rq, rk, rv, rg, rkc, rvc, dec_lane, dec_wide, dec_row, row2(g_ret[l]))
    y_mla = _attention(q, kx, kc, vtx, vtc)

    d_ret = RET_HEADS * RET_DV
    w_out_b = w_out[l].astype(BF16)
    return _tail(x, y_ret, y_mla, mod_x, row2(g_ffn[l]), row2(g_final), w_out_b[:d_ret], w_out_b[d_ret:],
                 w_ff1[l].astype(BF16), w_ff2[l].astype(BF16))
```

```python
import functools
import math

import jax
import jax.numpy as jnp
from jax import lax
from jax.experimental import pallas as pl
from jax.experimental.pallas import tpu as pltpu

F32 = jnp.float32
BF16 = jnp.bfloat16

D_MODEL = 1024
GRID_W = 64
RET_HEADS = 4
RET_DK = 64
RET_DV = 128
RET_CHUNK = 128
MLA_HEADS = 4
MLA_NOPE = 128
MLA_ROPE = 64
MLA_V = 128
MLA_QK = MLA_NOPE + MLA_ROPE
Q_LORA = 384
KV_LORA = 256
D_FF = 4 * D_MODEL
ROPE_BASE = 10000.0
EPS = 1e-6

LANES = 128

_IN_WIDTHS = (("rq", RET_HEADS * RET_DK), ("rk", RET_HEADS * RET_DK), ("rv", RET_HEADS * RET_DV),
              ("rg", RET_HEADS * RET_DV), ("cq", Q_LORA), ("ckv", KV_LORA), ("kpe", MLA_ROPE))
_IN_SRC = {}
_start = 0
for _name, _width in _IN_WIDTHS:
    _IN_SRC[_name] = (_start, _width)
    _start += _width


def _in_layout(names):
    cols, start = {}, 0
    for name in names:
        cols[name] = start
        start += _IN_SRC[name][1]
    return cols, ((start + LANES - 1) // LANES) * LANES


_LATENT_NAMES = ("rq", "rk", "rv", "rg", "cq", "ckv", "kpe")
_CTX_NAMES = ("rk", "rv", "ckv", "kpe")

PROJ_TILE = 1024
PROJ_SUB = 512
ATTN_TILE = 1024
ATTN_SUB = 256
RET_UNROLL = 8
TAIL_TILE = 512
FF_CHUNK = 1024
VMEM_LIMIT = 56 * 1024 * 1024


def _rms(x, g):
    return x * lax.rsqrt(jnp.mean(x * x, axis=-1, keepdims=True) + EPS) * g


def _silu(x):
    return x * jax.nn.sigmoid(x)


def _log_sigmoid(x):
    return jnp.minimum(x, 0.0) - jnp.log(1.0 + jnp.exp(-jnp.abs(x)))


def _swap_halves(x):
    lane = lax.broadcasted_iota(jnp.int32, x.shape, 1)
    first = (lane & (RET_DK - 1)) < (RET_DK // 2)
    return jnp.where(first, pltpu.roll(x, LANES - RET_DK // 2, axis=1), pltpu.roll(x, RET_DK // 2, axis=1))


def _rope(x, cos, sin):
    return x * cos + _swap_halves(x) * sin


def _adaln_kernel(c_ref, w_ref, b_ref, o_ref):
    s = _silu(c_ref[...]).astype(BF16)
    o_ref[...] = jnp.dot(s, w_ref[...].astype(BF16), preferred_element_type=F32) + b_ref[...]


def _adaln(cc, w_ada, b_ada):
    rows, d = cc.shape
    n = w_ada.shape[1]
    tn = 1024
    return pl.pallas_call(
        _adaln_kernel,
        out_shape=jax.ShapeDtypeStruct((rows, n), F32),
        grid=(n // tn,),
        in_specs=[pl.BlockSpec((rows, d), lambda j: (0, 0)),
                  pl.BlockSpec((d, tn), lambda j: (0, j)),
                  pl.BlockSpec((1, tn), lambda j: (0, j))],
        out_specs=pl.BlockSpec((rows, tn), lambda j: (0, j)),
        compiler_params=pltpu.CompilerParams(dimension_semantics=("arbitrary",)),
        name="adaln",
    )(cc, w_ada, b_ada)


def _proj_rows(latent, rows, refs):
    x_ref, mod_ref, gattn_ref, win_ref, gq_ref, wuq_ref, gkv_ref, wuk_ref, wuvt_ref = refs[:9]
    if latent:
        cos_ref, sin_ref, rq_ref, rk_ref, rv_ref, rg_ref, q_ref, k_ref, vt_ref = refs[9:]
    else:
        rk_ref, rv_ref, k_ref, vt_ref = refs[9:]
    cols, _ = _in_layout(_LATENT_NAMES if latent else _CTX_NAMES)
    x = x_ref[0, rows, :]
    mod = mod_ref[0]
    gain = gattn_ref[...] * (1.0 + mod[1:2])
    h = x * lax.rsqrt(jnp.mean(x * x, axis=-1, keepdims=True) + EPS) * gain + mod[0:1]
    p = jnp.dot(h.astype(BF16), win_ref[...], preferred_element_type=F32)

    if latent:
        cos = cos_ref[rows, :]
        sin = sin_ref[rows, :]
        lane = lax.broadcasted_iota(jnp.int32, cos.shape, 1)
        low = lane < MLA_ROPE
        cos_lo, sin_lo = jnp.where(low, cos, 1.0), jnp.where(low, sin, 0.0)
        cos_hi, sin_hi = jnp.where(low, 1.0, cos), jnp.where(low, 0.0, sin)

    k_scale = RET_DK ** -0.5
    for blk in range(RET_HEADS * RET_DK // LANES):
        rk = p[:, cols["rk"] + blk * LANES:cols["rk"] + (blk + 1) * LANES]
        if latent:
            rq = p[:, cols["rq"] + blk * LANES:cols["rq"] + (blk + 1) * LANES]
            rq_ref[0, rows, blk * LANES:(blk + 1) * LANES] = _rope(rq, cos, sin).astype(BF16)
            rk = _rope(rk, cos, sin)
        rk_ref[0, rows, blk * LANES:(blk + 1) * LANES] = (rk * k_scale).astype(BF16)
    rv_ref[0, rows, :] = p[:, cols["rv"]:cols["rv"] + RET_HEADS * RET_DV].astype(BF16)
    if latent:
        rg_ref[0, rows, :] = p[:, cols["rg"]:cols["rg"] + RET_HEADS * RET_DV].astype(BF16)

    ckv = _rms(p[:, cols["ckv"]:cols["ckv"] + KV_LORA], gkv_ref[...]).astype(BF16)
    kn = jnp.dot(ckv, wuk_ref[...], preferred_element_type=F32)
    vt_ref[0, :, rows] = lax.dot_general(wuvt_ref[...], ckv, (((1,), (1,)), ((), ())),
                                         preferred_element_type=F32).astype(BF16)
    kpe = p[:, cols["kpe"]:cols["kpe"] + LANES]
    if latent:
        kpe = _rope(kpe, cos, sin)
    kpe = kpe[:, :MLA_ROPE].astype(BF16)
    for hd in range(MLA_HEADS):
        k_ref[0, hd, rows, 0:MLA_NOPE] = kn[:, hd * MLA_NOPE:(hd + 1) * MLA_NOPE].astype(BF16)
        k_ref[0, hd, rows, MLA_NOPE:MLA_QK] = kpe

    if latent:
        cq = _rms(p[:, cols["cq"]:cols["cq"] + Q_LORA], gq_ref[...]).astype(BF16)
        q = jnp.dot(cq, wuq_ref[...], preferred_element_type=F32)
        q_scale = math.log2(math.e) / math.sqrt(MLA_QK)
        blocks = []
        for blk in range(MLA_HEADS * MLA_QK // LANES):
            qb = q[:, blk * LANES:(blk + 1) * LANES]
            lo_col = blk * LANES
            pe_lo = (lo_col % MLA_QK) == MLA_NOPE
            pe_hi = ((lo_col + MLA_ROPE) % MLA_QK) == MLA_NOPE
            if pe_lo:
                qb = _rope(qb, cos_lo, sin_lo)
            elif pe_hi:
                qb = _rope(qb, cos_hi, sin_hi)
            blocks.append(qb * q_scale)
        qr = jnp.concatenate(blocks, axis=1)
        for hd in range(MLA_HEADS):
            q_ref[0, hd, rows, :] = qr[:, hd * MLA_QK:(hd + 1) * MLA_QK].astype(BF16)


def _proj_body(latent, sub, *refs):
    tm = refs[0].shape[1]
    for r in range(0, tm, sub):
        _proj_rows(latent, slice(r, r + sub), refs)


def _proj(latent, x, mod, g_attn, w_in, g_q, w_uq, g_kv, w_uk, w_uvt, cos=None, sin=None):
    B, L, D = x.shape
    tm = PROJ_TILE if latent else L
    nt = L // tm
    hq = MLA_HEADS * MLA_QK
    hv = MLA_HEADS * MLA_V
    const = lambda b, t: (0, 0)
    if latent:
        mod_map = lambda b, t: (b, 0, 0)
    else:
        mod_map = lambda b, t: (0, 0, 0)
    in_specs = [
        pl.BlockSpec((1, tm, D), lambda b, t: (b, t, 0)),
        pl.BlockSpec((1, 6, D), mod_map),
        pl.BlockSpec((1, D), const),
        pl.BlockSpec(w_in.shape, const),
        pl.BlockSpec((1, Q_LORA), const),
        pl.BlockSpec((Q_LORA, hq), const),
        pl.BlockSpec((1, KV_LORA), const),
        pl.BlockSpec((KV_LORA, MLA_HEADS * MLA_NOPE), const),
        pl.BlockSpec((hv, KV_LORA), const),
    ]
    args = [x, mod, g_attn, w_in, g_q, w_uq, g_kv, w_uk, w_uvt]
    tok = lambda w: pl.BlockSpec((1, tm, w), lambda b, t: (b, t, 0))
    tok_s = lambda w: jax.ShapeDtypeStruct((B, L, w), BF16)
    head = pl.BlockSpec((1, MLA_HEADS, tm, MLA_QK), lambda b, t: (b, 0, t, 0))
    head_s = jax.ShapeDtypeStruct((B, MLA_HEADS, L, MLA_QK), BF16)
    vt = pl.BlockSpec((1, hv, tm), lambda b, t: (b, 0, t))
    vt_s = jax.ShapeDtypeStruct((B, hv, L), BF16)
    dk, dv = RET_HEADS * RET_DK, RET_HEADS * RET_DV
    if latent:
        in_specs += [pl.BlockSpec((tm, LANES), lambda b, t: (t, 0)),
                     pl.BlockSpec((tm, LANES), lambda b, t: (t, 0))]
        args += [cos, sin]
        out_specs = [tok(dk), tok(dk), tok(dv), tok(dv), head, head, vt]
        out_shape = [tok_s(dk), tok_s(dk), tok_s(dv), tok_s(dv), head_s, head_s, vt_s]
    else:
        out_specs = [tok(dk), tok(dv), head, vt]
        out_shape = [tok_s(dk), tok_s(dv), head_s, vt_s]
    return pl.pallas_call(
        functools.partial(_proj_body, latent, min(PROJ_SUB, tm)),
        out_shape=out_shape,
        grid=(B, nt),
        in_specs=in_specs,
        out_specs=out_specs,
        compiler_params=pltpu.CompilerParams(dimension_semantics=("arbitrary", "arbitrary"),
                                             vmem_limit_bytes=VMEM_LIMIT),
        name="proj_latent" if latent else "proj_ctx",
    )(*args)


def _ret_kernel(q_ref, k_ref, v_ref, g_ref, kc_ref, vc_ref, dl_ref, dw_ref, dr_ref, gret_ref,
                o_ref, kv_scr, s_scr):
    C = RET_CHUNK
    n_chunks = q_ref.shape[1] // C
    n_ctx = kc_ref.shape[1] // C
    two_dk = 2 * RET_DK
    two_dv = 2 * RET_DV

    lg_l = _log_sigmoid(dl_ref[0])
    lg_w = _log_sigmoid(dw_ref[0])
    lg_r = _log_sigmoid(dr_ref[0])
    pos = lax.broadcasted_iota(jnp.int32, (C, two_dk), 0).astype(F32)
    wq_f = jnp.exp(lg_l[0:1] * (pos + 1.0))
    wq_b = jnp.exp(lg_l[1:2] * (C - pos))
    wk_f = jnp.exp(lg_l[0:1] * (C - 1.0 - pos))
    wk_b = jnp.exp(lg_l[1:2] * pos)
    ri = lax.broadcasted_iota(jnp.int32, (C, 2 * C), 0)
    cj = lax.broadcasted_iota(jnp.int32, (C, 2 * C), 1) & (C - 1)
    diff = (ri - cj).astype(F32)
    dec = (jnp.where(diff >= 0, jnp.exp(lg_w[0:1] * jnp.maximum(diff, 0.0)), 0.0)
           + jnp.where(diff <= 0, jnp.exp(lg_w[1:2] * jnp.maximum(-diff, 0.0)), 0.0))
    gamma = jnp.exp(lg_r * float(C))

    srow = lax.broadcasted_iota(jnp.int32, (2 * two_dk, two_dv), 0)
    scol = lax.broadcasted_iota(jnp.int32, (2 * two_dk, two_dv), 1)
    diag = ((srow & (two_dk - 1)) < RET_DK) == (scol < RET_DV)
    fwd_rows = srow < two_dk
    k_lo = lax.broadcasted_iota(jnp.int32, (C, two_dk), 1) < RET_DK
    v_lo = lax.broadcasted_iota(jnp.int32, (C, two_dv), 1) < RET_DV

    def chunk_kv(kch, vch):
        kf = kch.astype(F32)
        kcat = jnp.concatenate([(kf * wk_f).astype(BF16), (kf * wk_b).astype(BF16)], axis=1)
        kv = lax.dot_general(kcat, vch, (((0,), (0,)), ((), ())), preferred_element_type=F32)
        return jnp.where(diag, kv, 0.0)

    ctx_kv = [chunk_kv(kc_ref[0, c * C:(c + 1) * C, :], vc_ref[0, c * C:(c + 1) * C, :]) for c in range(n_ctx)]
    state = jnp.zeros((2 * two_dk, two_dv), F32)
    for s in range(n_ctx):
        state = gamma * state + jnp.where(fwd_rows, ctx_kv[s], ctx_kv[n_ctx - 1 - s])

    def kv_step(g, carry):
        for u in range(RET_UNROLL):
            n = g * RET_UNROLL + u
            r = pl.multiple_of(n * C, C)
            kv_scr[n] = chunk_kv(k_ref[0, pl.ds(r, C), :], v_ref[0, pl.ds(r, C), :])
        return carry

    lax.fori_loop(0, n_chunks // RET_UNROLL, kv_step, 0)

    def scan_step(s, st):
        sb = st.astype(BF16)
        s_scr[s, 0:two_dk, :] = sb[0:two_dk]
        s_scr[n_chunks - 1 - s, two_dk:2 * two_dk, :] = sb[two_dk:2 * two_dk]
        return gamma * st + jnp.where(fwd_rows, kv_scr[s], kv_scr[n_chunks - 1 - s])

    lax.fori_loop(0, n_chunks, scan_step, state)

    gret = gret_ref[...]

    def out_chunk(n):
        r = pl.multiple_of(n * C, C)
        qch = q_ref[0, pl.ds(r, C), :]
        kch = k_ref[0, pl.ds(r, C), :]
        vch = v_ref[0, pl.ds(r, C), :]
        zk = jnp.zeros_like(kch)
        kbd = jnp.concatenate([jnp.where(k_lo, kch, zk), jnp.where(k_lo, zk, kch)], axis=0)
        sc = lax.dot_general(qch, kbd, (((1,), (1,)), ((), ())), preferred_element_type=F32)
        pm = (sc * dec).astype(BF16)
        zv = jnp.zeros_like(vch)
        vbd = jnp.concatenate([jnp.where(v_lo, vch, zv), jnp.where(v_lo, zv, vch)], axis=0)
        qf = qch.astype(F32)
        qcat = jnp.concatenate([(qf * wq_f).astype(BF16), (qf * wq_b).astype(BF16)], axis=1)
        o = (jnp.dot(pm, vbd, preferred_element_type=F32)
             + jnp.dot(qcat, s_scr[n], preferred_element_type=F32))
        gate = g_ref[0, pl.ds(r, C), :].astype(F32)
        for hh in range(2):
            oh = o[:, hh * RET_DV:(hh + 1) * RET_DV]
            mu = jnp.mean(oh, axis=-1, keepdims=True)
            ctr = oh - mu
            var = jnp.mean(ctr * ctr, axis=-1, keepdims=True)
            y = ctr * lax.rsqrt(var + EPS) * gret[:, hh * RET_DV:(hh + 1) * RET_DV]
            y = y * _silu(gate[:, hh * RET_DV:(hh + 1) * RET_DV])
            o_ref[0, pl.ds(r, C), hh * RET_DV:(hh + 1) * RET_DV] = y.astype(BF16)

    def out_step(g, carry):
        for u in range(RET_UNROLL):
            out_chunk(g * RET_UNROLL + u)
        return carry

    lax.fori_loop(0, n_chunks // RET_UNROLL, out_step, 0)


def _retention(rq, rk, rv, rg, rkc, rvc, dec_lane, dec_wide, dec_row, g_ret):
    B, L, _ = rq.shape
    Lc = rkc.shape[1]
    pairs = RET_HEADS // 2
    two_dk, two_dv = 2 * RET_DK, 2 * RET_DV
    n_chunks = L // RET_CHUNK
    return pl.pallas_call(
        _ret_kernel,
        out_shape=jax.ShapeDtypeStruct((B, L, RET_HEADS * RET_DV), BF16),
        grid=(B, pairs),
        in_specs=[
            pl.BlockSpec((1, L, two_dk), lambda b, p: (b, 0, p)),
            pl.BlockSpec((1, L, two_dk), lambda b, p: (b, 0, p)),
            pl.BlockSpec((1, L, two_dv), lambda b, p: (b, 0, p)),
            pl.BlockSpec((1, L, two_dv), lambda b, p: (b, 0, p)),
            pl.BlockSpec((1, Lc, two_dk), lambda b, p: (b, 0, p)),
            pl.BlockSpec((1, Lc, two_dv), lambda b, p: (b, 0, p)),
            pl.BlockSpec((1, 2, two_dk), lambda b, p: (p, 0, 0)),
            pl.BlockSpec((1, 2, two_dv), lambda b, p: (p, 0, 0)),
            pl.BlockSpec((1, 2 * two_dk, 1), lambda b, p: (p, 0, 0)),
            pl.BlockSpec((1, two_dv), lambda b, p: (0, p)),
        ],
        out_specs=pl.BlockSpec((1, L, two_dv), lambda b, p: (b, 0, p)),
        scratch_shapes=[pltpu.VMEM((n_chunks, 2 * two_dk, two_dv), F32),
                        pltpu.VMEM((n_chunks, 2 * two_dk, two_dv), BF16)],
        compiler_params=pltpu.CompilerParams(dimension_semantics=("arbitrary", "arbitrary"),
                                             vmem_limit_bytes=VMEM_LIMIT),
        name="retention",
    )(rq, rk, rv, rg, rkc, rvc, dec_lane, dec_wide, dec_row, g_ret)


def _attn_kernel(q_ref, kx_ref, kc_ref, vtx_ref, vtc_ref, o_ref, s_scr):
    nt = (((1,), (1,)), ((), ()))
    n_sub = q_ref.shape[2] // ATTN_SUB
    lx = kx_ref.shape[2]

    def scores(s):
        q = q_ref[0, 0, s * ATTN_SUB:(s + 1) * ATTN_SUB, :]
        sx = lax.dot_general(kx_ref[0, 0], q, nt, preferred_element_type=F32)
        sc = lax.dot_general(kc_ref[0, 0], q, nt, preferred_element_type=F32)
        s_scr[s % 2, 0:lx] = sx
        s_scr[s % 2, lx:] = sc
        return jnp.maximum(jnp.max(sx, axis=0, keepdims=True), jnp.max(sc, axis=0, keepdims=True))

    m_next = scores(0)
    for s in range(n_sub):
        m = m_next
        if s + 1 < n_sub:
            m_next = scores(s + 1)
        p = jnp.exp2(s_scr[s % 2] - m)
        den = jnp.sum(p, axis=0, keepdims=True)
        pb = p.astype(BF16)
        ot = (jnp.dot(vtx_ref[0], pb[0:lx], preferred_element_type=F32)
              + jnp.dot(vtc_ref[0], pb[lx:], preferred_element_type=F32))
        o_ref[0, s * ATTN_SUB:(s + 1) * ATTN_SUB, :] = (ot * (1.0 / den)).T.astype(BF16)


def _attention(q, kx, kc, vtx, vtc):
    B, H, L, dqk = q.shape
    Lc = kc.shape[2]
    tq = ATTN_TILE
    return pl.pallas_call(
        _attn_kernel,
        out_shape=jax.ShapeDtypeStruct((B, L, H * MLA_V), BF16),
        grid=(B, H, L // tq),
        in_specs=[
            pl.BlockSpec((1, 1, tq, dqk), lambda b, h, i: (b, h, i, 0)),
            pl.BlockSpec((1, 1, L, dqk), lambda b, h, i: (b, h, 0, 0)),
            pl.BlockSpec((1, 1, Lc, dqk), lambda b, h, i: (b, h, 0, 0)),
            pl.BlockSpec((1, MLA_V, L), lambda b, h, i: (b, h, 0)),
            pl.BlockSpec((1, MLA_V, Lc), lambda b, h, i: (b, h, 0)),
        ],
        out_specs=pl.BlockSpec((1, tq, MLA_V), lambda b, h, i: (b, i, h)),
        scratch_shapes=[pltpu.VMEM((2, L + Lc, ATTN_SUB), F32)],
        compiler_params=pltpu.CompilerParams(dimension_semantics=("arbitrary", "arbitrary", "arbitrary"),
                                             vmem_limit_bytes=VMEM_LIMIT),
        name="attention",
    )(q, kx, kc, vtx, vtc)


def _tail_kernel(x_ref, yr_ref, ym_ref, mod_ref, gffn_ref, gfin_ref, wo_r_ref, wo_m_ref, w1_ref, w2_ref, o_ref):
    x = x_ref[0]
    mod = mod_ref[0]
    mix = (jnp.dot(yr_ref[0], wo_r_ref[...], preferred_element_type=F32)
           + jnp.dot(ym_ref[0], wo_m_ref[...], preferred_element_type=F32))
    x_mid = x + mod[2:3] * mix
    h = (_rms(x_mid, gffn_ref[...]) * (1.0 + mod[4:5]) + mod[3:4]).astype(BF16)
    acc = jnp.zeros_like(x_mid)
    for c in range(D_FF // FF_CHUNK):
        u = jnp.dot(h, w1_ref[:, c * FF_CHUNK:(c + 1) * FF_CHUNK], preferred_element_type=F32)
        r = jnp.square(jnp.maximum(u, 0.0)).astype(BF16)
        acc = acc + jnp.dot(r, w2_ref[c * FF_CHUNK:(c + 1) * FF_CHUNK, :], preferred_element_type=F32)
    x_out = x_mid + mod[5:6] * acc
    o_ref[0] = _rms(x_out, gfin_ref[...])


def _tail(x, y_ret, y_mla, mod, g_ffn, g_final, wo_r, wo_m, w1, w2):
    B, L, D = x.shape
    tm = TAIL_TILE
    const = lambda b, t: (0, 0)
    resident = lambda shape: pl.BlockSpec(shape, const, pipeline_mode=pl.Buffered(1))
    dmix = y_ret.shape[2]
    return pl.pallas_call(
        _tail_kernel,
        out_shape=jax.ShapeDtypeStruct((B, L, D), F32),
        grid=(B, L // tm),
        in_specs=[
            pl.BlockSpec((1, tm, D), lambda b, t: (b, t, 0)),
            pl.BlockSpec((1, tm, dmix), lambda b, t: (b, t, 0)),
            pl.BlockSpec((1, tm, dmix), lambda b, t: (b, t, 0)),
            pl.BlockSpec((1, 6, D), lambda b, t: (b, 0, 0)),
            pl.BlockSpec((1, D), const),
            pl.BlockSpec((1, D), const),
            resident(wo_r.shape),
            resident(wo_m.shape),
            resident(w1.shape),
            resident(w2.shape),
        ],
        out_specs=pl.BlockSpec((1, tm, D), lambda b, t: (b, t, 0)),
        compiler_params=pltpu.CompilerParams(dimension_semantics=("arbitrary", "arbitrary"),
                                             vmem_limit_bytes=VMEM_LIMIT),
        name="tail",
    )(x, y_ret, y_mla, mod, g_ffn, g_final, wo_r, wo_m, w1, w2)


def _rope_tables(L):
    rows = L // GRID_W
    row = jnp.repeat(jnp.arange(rows, dtype=F32), GRID_W)
    col = jnp.tile(jnp.arange(GRID_W, dtype=F32), rows)
    n_freq = RET_DK // 4
    freq = ROPE_BASE ** (-jnp.arange(n_freq, dtype=F32) / n_freq)
    ang = jnp.concatenate([row[:, None] * freq, col[:, None] * freq], axis=-1)
    cos, sin = jnp.cos(ang), jnp.sin(ang)
    reps = LANES // RET_DK
    return (jnp.tile(jnp.concatenate([cos, cos], axis=-1), (1, reps)),
            jnp.tile(jnp.concatenate([-sin, sin], axis=-1), (1, reps)))


def _decay_layouts(dec_f, dec_b):
    pairs = RET_HEADS // 2
    d = jnp.stack([dec_f, dec_b]).astype(F32).reshape(2, pairs, 2)
    d = jnp.transpose(d, (1, 0, 2))
    lane = jnp.repeat(d, RET_DK, axis=2)
    wide = jnp.repeat(d, RET_DV, axis=2)
    row = lane.reshape(pairs, 4 * RET_DK, 1)
    return lane, wide, row


def kernel(x, c, ctx, c_ctx, w_ada, b_ada, g_attn, g_ffn, w_in, ret_decay_fwd, ret_decay_bwd, g_ret,
           g_q_lora, w_uq, g_kv_lora, w_ukv, w_out, w_ff1, w_ff2, g_final):
    B, L, D = x.shape
    depth = w_ada.shape[0]
    assert depth == 1, "context-stream update between layers is not implemented"
    cos, sin = _rope_tables(L)
    row2 = lambda v: v.reshape(1, -1)
    l = 0
    rows = ((B + 1 + 7) // 8) * 8
    cc = jnp.zeros((rows, D), F32).at[:B].set(c).at[B].set(c_ctx)
    mod = _adaln(cc, w_ada[l], row2(b_ada[l]))
    mod_x = mod[:B].reshape(B, 6, D)
    mod_c = mod[B:B + 1].reshape(1, 6, D)

    def in_weight(names):
        _, padded = _in_layout(names)
        w = jnp.concatenate([w_in[l][:, _IN_SRC[n][0]:_IN_SRC[n][0] + _IN_SRC[n][1]] for n in names], axis=1)
        return jnp.pad(w, ((0, 0), (0, padded - w.shape[1]))).astype(BF16)

    w_uq_b = w_uq[l].astype(BF16)
    w_ukv_h = w_ukv[l].astype(BF16).reshape(KV_LORA, MLA_HEADS, MLA_NOPE + MLA_V)
    w_uk = w_ukv_h[:, :, :MLA_NOPE].reshape(KV_LORA, MLA_HEADS * MLA_NOPE)
    w_uvt = w_ukv_h[:, :, MLA_NOPE:].reshape(KV_LORA, MLA_HEADS * MLA_V).T
    shared = (row2(g_q_lora[l]), w_uq_b, row2(g_kv_lora[l]), w_uk, w_uvt)
    rq, rk, rv, rg, q, kx, vtx = _proj(True, x, mod_x, row2(g_attn[l]), in_weight(_LATENT_NAMES), *shared, cos, sin)
    rkc, rvc, kc, vtc = _proj(False, ctx, mod_c, row2(g_attn[l]), in_weight(_CTX_NAMES), *shared)

    dec_lane, dec_wide, dec_row = _decay_layouts(ret_decay_fwd[l], ret_decay_bwd[l])
    y_ret = _retention(rq, rk, rv, rg, rkc, rvc, dec_lane, dec_wide, dec_row, row2(g_ret[l]))
    y_mla = _attention(q, kx, kc, vtx, vtc)

    d_ret = RET_HEADS * RET_DV
    w_out_b = w_out[l].astype(BF16)
    return _tail(x, y_ret, y_mla, mod_x, row2(g_ffn[l]), row2(g_final), w_out_b[:d_ret], w_out_b[d_ret:],
                 w_ff1[l].astype(BF16), w_ff2[l].astype(BF16))
```
